```python
import math
import jax
import jax.numpy as jnp
from jax import lax
import numpy as np

D_MODEL = 2048
BATCH = 4
SEQ = 2048
DEPTH = 2

GRID_W = 64
CTX_LEN = 256
N_MOD = 9
N_NORMS = 6
D_FF = 5504
FFN_RES_WEIGHT = 0.5
EPS = 1e-6

ATT_HEADS = 8
ATT_QK = 64
ATT_V = 2 * ATT_QK
ATT_BLOCK = 128
ROPE_THETA = 10000.0
ROPE_PAIRS = ATT_QK // 4
LAMBDA_INIT_BASE = 0.8
LAMBDA_INIT_AMP = 0.6
LAMBDA_INIT_RATE = 0.3

REC_HEADS = 8
REC_DK = 128
REC_DV = 128
REC_CHUNK = 64

CONV_WIDTH = D_MODEL
CONV_K = 3

N_EVEN = (DEPTH + 1) // 2
N_ODD = DEPTH // 2
ATT_QK_COLS = ATT_HEADS * 2 * ATT_QK
ATT_V_COLS = ATT_HEADS * ATT_V
REC_K_COLS = REC_HEADS * REC_DK
REC_V_COLS = REC_HEADS * REC_DV
MIX_SPLITS = (ATT_QK_COLS, ATT_QK_COLS, ATT_V_COLS, REC_K_COLS, REC_K_COLS, REC_K_COLS, REC_V_COLS, REC_V_COLS)
MIX_IN = 2 * ATT_QK_COLS + ATT_V_COLS + 3 * REC_K_COLS + 2 * REC_V_COLS
MIX_OUT = ATT_V_COLS + REC_V_COLS

kernel_name = 'hybrid_diffattn_hgrn2_shortconv_dit'


def _rms_norm(x, g):
    xf = x.astype(jnp.float32)
    y = xf * lax.rsqrt(jnp.mean(xf * xf, axis=-1, keepdims=True) + EPS)
    return (y * g.astype(jnp.float32)).astype(x.dtype)


def _modulation(cond, w, b):
    m = jax.nn.silu(cond) @ w + b
    return jnp.split(m[..., None, :], N_MOD, axis=-1)


def _modulate(x, g, shift, scale):
    return _rms_norm(x, g) * (1 + scale) + shift


def _residual(x, y, g, gate, weight):
    return x + weight * gate * _rms_norm(y, g)


def _swiglu(h, w_gate, w_up, w_down):
    return (jax.nn.silu(h @ w_gate) * (h @ w_up)) @ w_down


def _ffn_half(x, mods, g_pre, g_post, w_gate, w_up, w_down):
    shift, scale, gate = mods
    y = _swiglu(_modulate(x, g_pre, shift, scale), w_gate, w_up, w_down)
    return _residual(x, y, g_post, gate, FFN_RES_WEIGHT)


def _axial_rope(t, n):
    rows = n // GRID_W
    row = jnp.broadcast_to(jnp.arange(rows, dtype=jnp.float32)[:, None], (rows, GRID_W)).reshape(n)
    col = jnp.broadcast_to(jnp.arange(GRID_W, dtype=jnp.float32)[None, :], (rows, GRID_W)).reshape(n)
    freqs = ROPE_THETA ** (-jnp.arange(ROPE_PAIRS, dtype=jnp.float32) / ROPE_PAIRS)

    def rot(part, pos):
        ang = (pos[:, None] * freqs)[:, None, None, :]
        cos, sin = jnp.cos(ang), jnp.sin(ang)
        p = part.astype(jnp.float32)
        p1, p2 = p[..., :ROPE_PAIRS], p[..., ROPE_PAIRS:]
        return jnp.concatenate([p1 * cos - p2 * sin, p2 * cos + p1 * sin], axis=-1)

    half = ATT_QK // 2
    out = jnp.concatenate([rot(t[..., :half], row), rot(t[..., half:], col)], axis=-1)
    return out.astype(t.dtype)


def _diff_attend(q, k, v, lam):
    s = jnp.einsum('bqhmd,bkhmd->bhmqk', q, k).astype(jnp.float32) * (ATT_QK ** -0.5)
    p = jax.nn.softmax(s, axis=-1)
    w = (p[:, :, 0] - lam * p[:, :, 1]).astype(v.dtype)
    return jnp.einsum('bhqk,bkhv->bqhv', w, v)


def _diff_attention(q_lat, k_lat, v_lat, q_ctx, k_ctx, v_ctx, lam, with_ctx_out):
    k_all = jnp.concatenate([k_ctx, k_lat], axis=1)
    v_all = jnp.concatenate([v_ctx, v_lat], axis=1)
    b, n = q_lat.shape[:2]
    nblk = n // ATT_BLOCK
    qb = jnp.moveaxis(q_lat.reshape(b, nblk, ATT_BLOCK, ATT_HEADS, 2, ATT_QK), 1, 0)
    o = lax.map(lambda qi: _diff_attend(qi, k_all, v_all, lam), qb)
    o_lat = jnp.moveaxis(o, 0, 1).reshape(b, n, ATT_HEADS, ATT_V)
    o_ctx = _diff_attend(q_ctx, k_ctx, v_ctx, lam) if with_ctx_out else None
    return o_lat, o_ctx


def _gates(z, lb):
    lb = lb.reshape(REC_HEADS, REC_DK)
    f = lb + (1.0 - lb) * jax.nn.sigmoid(z)
    return 1.0 - f, jnp.log(f)


def _gla_chunked(q, k, v, log_f, state, with_output):
    b, n, h, dk = q.shape
    nc = n // REC_CHUNK

    def chunks(a):
        return jnp.moveaxis(a.reshape(b, nc, REC_CHUNK, h, a.shape[-1]), 1, 0)

    lower = jnp.tril(jnp.ones((REC_CHUNK, REC_CHUNK), dtype=bool))[None, :, :, None, None]

    def step(s, inp):
        qc, kc, vc, gc = inp
        cum = jnp.cumsum(gc, axis=1)
        last = cum[:, -1]
        s_new = s * jnp.exp(last)[..., None] + jnp.einsum('bshk,bshv->bhkv', kc * jnp.exp(last[:, None] - cum), vc)
        if not with_output:
            return s_new, None
        o_inter = jnp.einsum('bthk,bhkv->bthv', qc * jnp.exp(cum), s)
        decay = jnp.exp(jnp.where(lower, cum[:, :, None] - cum[:, None], -jnp.inf))
        att = jnp.einsum('bthk,bshk,btshk->bhts', qc, kc, decay)
        return s_new, o_inter + jnp.einsum('bhts,bshv->bthv', att, vc)

    s_final, o = lax.scan(step, state, (chunks(q), chunks(k), chunks(v), chunks(log_f)))
    if with_output:
        o = jnp.moveaxis(o, 0, 1).reshape(b, n, h, v.shape[-1])
    return o, s_final


def _hgrn2_bidir(rec_lat, rec_ctx, lb_fwd, lb_bwd, with_ctx_out):
    q_l, zf_l, zb_l, i_l = rec_lat
    q_c, zf_c, zb_c, i_c = rec_ctx
    s0 = jnp.zeros((q_l.shape[0], REC_HEADS, REC_DK, REC_DV), jnp.float32)
    flip = lambda a: jnp.flip(a, axis=1)
    k_c, g_c = _gates(zf_c, lb_fwd)
    k_l, g_l = _gates(zf_l, lb_fwd)
    oc_f, s_f = _gla_chunked(q_c, k_c, i_c, g_c, s0, with_ctx_out)
    ol_f, _ = _gla_chunked(q_l, k_l, i_l, g_l, s_f, True)
    k_c, g_c = _gates(flip(zb_c), lb_bwd)
    k_l, g_l = _gates(flip(zb_l), lb_bwd)
    oc_b, s_b = _gla_chunked(flip(q_c), k_c, flip(i_c), g_c, s0, with_ctx_out)
    ol_b, _ = _gla_chunked(flip(q_l), k_l, flip(i_l), g_l, s_b, True)
    o_lat = ol_f + flip(ol_b)
    o_ctx = oc_f + flip(oc_b) if with_ctx_out else None
    return o_lat, o_ctx


def _mixer_even(h_lat, h_ctx, w_in, w_out, lam_vecs, att_norm_g, rec_norm_g, lb_fwd, lb_bwd, lam_init, with_ctx_out):
    split_at = np.cumsum(MIX_SPLITS)[:-1].tolist()

    def project(h):
        b, n = h.shape[:2]
        qa, ka, va, qr, zf, zb, ir, gr = jnp.split(h @ w_in, split_at, axis=-1)
        att = (qa.reshape(b, n, ATT_HEADS, 2, ATT_QK), ka.reshape(b, n, ATT_HEADS, 2, ATT_QK),
               va.reshape(b, n, ATT_HEADS, ATT_V))
        f32 = lambda a, d: a.reshape(b, n, REC_HEADS, d).astype(jnp.float32)
        rec = (f32(qr, REC_DK) * (REC_DK ** -0.5), f32(zf, REC_DK), f32(zb, REC_DK), f32(ir, REC_DV))
        return att, rec, gr.reshape(b, n, REC_HEADS, REC_DV)

    (qa_l, ka_l, va_l), rec_l, g_l = project(h_lat)
    (qa_c, ka_c, va_c), rec_c, g_c = project(h_ctx)
    n = h_lat.shape[1]
    qa_l = _axial_rope(qa_l, n)
    ka_l = _axial_rope(ka_l, n)
    lv = lam_vecs.astype(jnp.float32)
    lam = jnp.exp(jnp.sum(lv[0] * lv[1])) - jnp.exp(jnp.sum(lv[2] * lv[3])) + lam_init
    oa_l, oa_c = _diff_attention(qa_l, ka_l, va_l, qa_c, ka_c, va_c, lam, with_ctx_out)
    or_l, or_c = _hgrn2_bidir(rec_l, rec_c, lb_fwd, lb_bwd, with_ctx_out)

    def merge(oa, orec, g):
        b, n = g.shape[:2]
        oa = _rms_norm(oa, att_norm_g) * (1.0 - lam_init)
        orec = (_rms_norm(orec, rec_norm_g) * jax.nn.silu(g.astype(jnp.float32))).astype(g.dtype)
        return jnp.concatenate([oa.reshape(b, n, ATT_V_COLS), orec.reshape(b, n, REC_V_COLS)], axis=-1) @ w_out

    y_lat = merge(oa_l, or_l, g_l)
    y_ctx = merge(oa_c, or_c, g_c) if with_ctx_out else None
    return y_lat, y_ctx


def _short_conv(h, w_in, conv_w, w_out):
    b_gate, c_gate, v = jnp.split(h @ w_in, 3, axis=-1)
    u = lax.conv_general_dilated(c_gate * v, conv_w[:, None, :], window_strides=(1,), padding='SAME',
                                 dimension_numbers=('NWC', 'WIO', 'NWC'), feature_group_count=CONV_WIDTH)
    return (b_gate * u) @ w_out


def setup_inputs(seed: int = 0) -> dict:
    key = jax.random.key(seed)
    ks = jax.random.split(key, 20)
    nrm = lambda k, shape, std: std * jax.random.normal(k, shape, jnp.float32)
    return {
        'x': nrm(ks[0], (BATCH, SEQ, D_MODEL), 1.0),
        'c': nrm(ks[1], (BATCH, D_MODEL), 1.0),
        'ctx': nrm(ks[2], (BATCH, CTX_LEN, D_MODEL), 1.0),
        'c_ctx': nrm(ks[3], (D_MODEL,), 1.0),
        'ada_w': nrm(ks[4], (DEPTH, D_MODEL, N_MOD * D_MODEL), 0.5 * D_MODEL ** -0.5),
        'ada_b': nrm(ks[5], (DEPTH, N_MOD * D_MODEL), 0.02),
        'norm_g': 1.0 + nrm(ks[6], (DEPTH, N_NORMS, D_MODEL), 0.02),
        'ffn_w_gate': nrm(ks[7], (DEPTH, 2, D_MODEL, D_FF), D_MODEL ** -0.5),
        'ffn_w_up': nrm(ks[8], (DEPTH, 2, D_MODEL, D_FF), D_MODEL ** -0.5),
        'ffn_w_down': nrm(ks[9], (DEPTH, 2, D_FF, D_MODEL), D_FF ** -0.5),
        'mix_w_in': nrm(ks[10], (N_EVEN, D_MODEL, MIX_IN), D_MODEL ** -0.5),
        'mix_w_out': nrm(ks[11], (N_EVEN, MIX_OUT, D_MODEL), MIX_OUT ** -0.5),
        'diff_lambda': nrm(ks[12], (N_EVEN, 4, ATT_QK), 0.1),
        'diff_norm_g': 1.0 + nrm(ks[13], (N_EVEN, ATT_V), 0.02),
        'rec_norm_g': 1.0 + nrm(ks[14], (N_EVEN, REC_DV), 0.02),
        'rec_lb': nrm(ks[15], (2, N_EVEN + 1, REC_K_COLS), 0.1),
        'conv_w_in': nrm(ks[16], (N_ODD, D_MODEL, 3 * CONV_WIDTH), D_MODEL ** -0.5),
        'conv_w': nrm(ks[17], (N_ODD, CONV_K, CONV_WIDTH), CONV_K ** -0.5),
        'conv_w_out': nrm(ks[18], (N_ODD, CONV_WIDTH, D_MODEL), CONV_WIDTH ** -0.5),
    }


def reference(x, c, ctx, c_ctx, ada_w, ada_b, norm_g, ffn_w_gate, ffn_w_up, ffn_w_down, mix_w_in, mix_w_out,
              diff_lambda, diff_norm_g, rec_norm_g, rec_lb, conv_w_in, conv_w, conv_w_out):
    x_ctx = ctx
    lb_all = jnp.cumsum(jax.nn.softmax(rec_lb.astype(jnp.float32), axis=1), axis=1)
    for l in range(DEPTH):
        even = l % 2 == 0
        ctx_out = any(j % 2 == 0 for j in range(l + 1, DEPTH))
        ctx_in = even or ctx_out
        g_n = norm_g[l]
        m_lat = _modulation(c, ada_w[l], ada_b[l])
        ffn1 = (ffn_w_gate[l, 0], ffn_w_up[l, 0], ffn_w_down[l, 0])
        ffn2 = (ffn_w_gate[l, 1], ffn_w_up[l, 1], ffn_w_down[l, 1])
        x = _ffn_half(x, m_lat[0:3], g_n[0], g_n[1], *ffn1)
        if ctx_in:
            m_ctx = _modulation(c_ctx, ada_w[l], ada_b[l])
            x_ctx = _ffn_half(x_ctx, m_ctx[0:3], g_n[0], g_n[1], *ffn1)
        h_lat = _modulate(x, g_n[2], m_lat[3], m_lat[4])
        if even:
            e = l // 2
            h_ctx = _modulate(x_ctx, g_n[2], m_ctx[3], m_ctx[4])
            lam_init = LAMBDA_INIT_BASE - LAMBDA_INIT_AMP * math.exp(-LAMBDA_INIT_RATE * l)
            y_lat, y_ctx = _mixer_even(h_lat, h_ctx, mix_w_in[e], mix_w_out[e], diff_lambda[e], diff_norm_g[e],
                                       rec_norm_g[e], lb_all[0, e], lb_all[1, e], lam_init, ctx_out)
        else:
            o = l // 2
            y_lat = _short_conv(h_lat, conv_w_in[o], conv_w[o], conv_w_out[o])
            y_ctx = None
            if ctx_out:
                h_ctx = _modulate(x_ctx, g_n[2], m_ctx[3], m_ctx[4])
                y_ctx = _short_conv(h_ctx, conv_w_in[o], conv_w[o], conv_w_out[o])
        x = _residual(x, y_lat, g_n[3], m_lat[5], 1.0)
        x = _ffn_half(x, m_lat[6:9], g_n[4], g_n[5], *ffn2)
        if ctx_out:
            x_ctx = _residual(x_ctx, y_ctx, g_n[3], m_ctx[5], 1.0)
            x_ctx = _ffn_half(x_ctx, m_ctx[6:9], g_n[4], g_n[5], *ffn2)
    return x
```

```python
import functools
import math

import jax
import jax.numpy as jnp
from jax import lax
from jax.experimental import pallas as pl
from jax.experimental.pallas import tpu as pltpu

F32 = jnp.float32
BF16 = jnp.bfloat16

EPS = 1e-6
N_MOD = 9
FFN_RES_WEIGHT = 0.5
GRID_W = 64
ATT_HEADS = 8
ATT_QK = 64
ATT_V = 2 * ATT_QK
ROPE_THETA = 10000.0
ROPE_PAIRS = ATT_QK // 4
LAMBDA_INIT_BASE = 0.8
LAMBDA_INIT_AMP = 0.6
LAMBDA_INIT_RATE = 0.3
REC_HEADS = 8
REC_DK = 128
REC_DV = 128
REC_CHUNK = 64
CONV_K = 3

LANES = 128
MIB = 1024 * 1024
MOD_ROWS = 8


def _params(sem, vmem_mib):
    return pltpu.CompilerParams(dimension_semantics=sem, vmem_limit_bytes=vmem_mib * MIB)


def _rms(y, g):
    return y * lax.rsqrt(jnp.mean(y * y, axis=-1, keepdims=True) + EPS) * g


def _dot(a, b):
    return jnp.dot(a, b, preferred_element_type=F32)


def _dot_nt(a, b):
    return lax.dot_general(a, b, (((1,), (1,)), ((), ())), preferred_element_type=F32)


def _dot_tn(a, b):
    return lax.dot_general(a, b, (((0,), (0,)), ((), ())), preferred_element_type=F32)


def _mod_kernel(c_ref, w_ref, b_ref, o_ref):
    cv = c_ref[...]
    s = (cv * jax.nn.sigmoid(cv)).astype(BF16)
    o_ref[...] = _dot(s, w_ref[...].astype(BF16)) + b_ref[...]


def _modulation(cvec, ada_w, ada_b, layer):
    depth, d, n = ada_w.shape
    tn = 1024
    b3 = ada_b.reshape(depth, 1, n)
    out = pl.pallas_call(
        _mod_kernel,
        grid=(n // tn,),
        in_specs=[
            pl.BlockSpec((MOD_ROWS, d), lambda j: (0, 0)),
            pl.BlockSpec((None, d, tn), lambda j: (layer, 0, j)),
            pl.BlockSpec((None, 1, tn), lambda j: (layer, 0, j)),
        ],
        out_specs=pl.BlockSpec((MOD_ROWS, tn), lambda j: (0, j)),
        out_shape=jax.ShapeDtypeStruct((MOD_ROWS, n), F32),
        compiler_params=_params(("arbitrary",), 40),
        name="modulation",
    )(cvec, ada_w, b3)
    return out.reshape(MOD_ROWS, N_MOD, d)


def _ffn_kernel(x_ref, mod_ref, g_ref, wg_ref, wu_ref, wd_ref, o_ref, h_ref, *, k0, gi, nj, tf, tail):
    j = pl.program_id(1)

    @pl.when(j == 0)
    def _():
        h = _rms(x_ref[...], g_ref[gi:gi + 1, :]) * (1.0 + mod_ref[k0 + 1:k0 + 2, :]) + mod_ref[k0:k0 + 1, :]
        h_ref[...] = h.astype(BF16)
        o_ref[...] = jnp.zeros_like(o_ref)

    def partial_down(width):
        h = h_ref[...]
        g = _dot(h, wg_ref[:, :width].astype(BF16))
        u = _dot(h, wu_ref[:, :width].astype(BF16))
        a = (g * jax.nn.sigmoid(g) * u).astype(BF16)
        return _dot(a, wd_ref[:width, :].astype(BF16))

    @pl.when(j < nj - 1)
    def _():
        o_ref[...] += partial_down(tf)

    @pl.when(j == nj - 1)
    def _():
        y = o_ref[...] + partial_down(tail)
        gate = mod_ref[k0 + 2:k0 + 3, :]
        o_ref[...] = x_ref[...] + (FFN_RES_WEIGHT * gate) * _rms(y, g_ref[gi + 1:gi + 2, :])


def _ffn_half(x, mod3, row_fn, g_l, wg, wu, wd, layer, half, k0, gi, tm, tf=256):
    t, d = x.shape
    f = wg.shape[-1]
    nj = pl.cdiv(f, tf)
    tail = f - (nj - 1) * tf
    kern = functools.partial(_ffn_kernel, k0=k0, gi=gi, nj=nj, tf=tf, tail=tail)
    return pl.pallas_call(
        kern,
        grid=(t // tm, nj),
        in_specs=[
            pl.BlockSpec((tm, d), lambda i, j: (i, 0), pipeline_mode=pl.Buffered(1)),
            pl.BlockSpec((None, N_MOD, d), lambda i, j: (row_fn(i), 0, 0)),
            pl.BlockSpec(g_l.shape, lambda i, j: (0, 0)),
            pl.BlockSpec((None, None, d, tf), lambda i, j: (layer, half, 0, j)),
            pl.BlockSpec((None, None, d, tf), lambda i, j: (layer, half, 0, j)),
            pl.BlockSpec((None, None, tf, d), lambda i, j: (layer, half, j, 0)),
        ],
        out_specs=pl.BlockSpec((tm, d), lambda i, j: (i, 0)),
        out_shape=jax.ShapeDtypeStruct((t, d), F32),
        scratch_shapes=[pltpu.VMEM((tm, d), BF16)],
        compiler_params=_params(("parallel", "arbitrary"), 56),
        name="ffn_half",
    )(x, mod3, g_l, wg, wu, wd)


def _proj_kernel(*refs, k0, gi, rope_blocks):
    if rope_blocks:
        x_ref, mod_ref, g_ref, w_ref, cos_ref, sin_ref, o_ref, h_ref = refs
    else:
        x_ref, mod_ref, g_ref, w_ref, o_ref, h_ref = refs
    j = pl.program_id(1)

    @pl.when(j == 0)
    def _():
        h = _rms(x_ref[...], g_ref[gi:gi + 1, :]) * (1.0 + mod_ref[k0 + 1:k0 + 2, :]) + mod_ref[k0:k0 + 1, :]
        h_ref[...] = h.astype(BF16)

    p = _dot(h_ref[...], w_ref[...].astype(BF16))
    if not rope_blocks:
        o_ref[...] = p
        return

    @pl.when(j < rope_blocks)
    def _():
        tn = p.shape[-1]
        lane = lax.broadcasted_iota(jnp.int32, (1, tn), 1)
        first = (lane % (2 * ROPE_PAIRS)) < ROPE_PAIRS
        partner = jnp.where(first, pltpu.roll(p, tn - ROPE_PAIRS, axis=1), pltpu.roll(p, ROPE_PAIRS, axis=1))
        o_ref[...] = p * cos_ref[...] + partner * sin_ref[...]

    @pl.when(j >= rope_blocks)
    def _():
        o_ref[...] = p


def _proj(x, mod3, row_fn, g_l, w, w_idx, k0, gi, tm, tn, rope=None):
    t, d = x.shape
    n = w.shape[-1]
    in_specs = [
        pl.BlockSpec((tm, d), lambda i, j: (i, 0)),
        pl.BlockSpec((None, N_MOD, d), lambda i, j: (row_fn(i), 0, 0)),
        pl.BlockSpec(g_l.shape, lambda i, j: (0, 0)),
        pl.BlockSpec((None, d, tn), lambda i, j: (w_idx, 0, j)),
    ]
    args = [x, mod3, g_l, w]
    rope_blocks = 0
    if rope is not None:
        cos_t, sin_t, rope_cols = rope
        seq_tiles = cos_t.shape[0] // tm
        rope_blocks = rope_cols // tn
        in_specs += [pl.BlockSpec((tm, tn), lambda i, j: (i % seq_tiles, 0))] * 2
        args += [cos_t, sin_t]
    kern = functools.partial(_proj_kernel, k0=k0, gi=gi, rope_blocks=rope_blocks)
    return pl.pallas_call(
        kern,
        grid=(t // tm, n // tn),
        in_specs=in_specs,
        out_specs=pl.BlockSpec((tm, tn), lambda i, j: (i, j)),
        out_shape=jax.ShapeDtypeStruct((t, n), F32),
        scratch_shapes=[pltpu.VMEM((tm, d), BF16)],
        compiler_params=_params(("parallel", "arbitrary"), 56),
        name="norm_proj",
    )(*args)


def _rope_tables(seq, reps):
    n = jnp.arange(seq)
    row = (n // GRID_W).astype(F32)
    col = (n % GRID_W).astype(F32)
    freqs = ROPE_THETA ** (-jnp.arange(ROPE_PAIRS, dtype=F32) / ROPE_PAIRS)

    def half(pos):
        ang = pos[:, None] * freqs
        c, s = jnp.cos(ang), jnp.sin(ang)
        return jnp.concatenate([c, c], axis=-1), jnp.concatenate([-s, s], axis=-1)

    cr, sr = half(row)
    cc, sc = half(col)
    cos64 = jnp.concatenate([cr, cc], axis=-1)
    sin64 = jnp.concatenate([sr, sc], axis=-1)
    return jnp.tile(cos64, (1, reps)), jnp.tile(sin64, (1, reps))


def _attn_kernel(q_ref, kl_ref, vl_ref, kc_ref, vc_ref, lv_ref, gn_ref, o_ref, *, lam_init):
    q = q_ref[...] * (ATT_QK ** -0.5)
    lane = lax.broadcasted_iota(jnp.int32, (1, 2 * ATT_QK), 1)
    q1 = jnp.where(lane < ATT_QK, q, 0.0).astype(BF16)
    q2 = jnp.where(lane >= ATT_QK, q, 0.0).astype(BF16)
    kl = kl_ref[...].astype(BF16)
    kc = kc_ref[...].astype(BF16)

    def probs(qm):
        sc = _dot_nt(qm, kc)
        sl = _dot_nt(qm, kl)
        m = jnp.maximum(jnp.max(sc, axis=-1, keepdims=True), jnp.max(sl, axis=-1, keepdims=True))
        ec = jnp.exp(sc - m)
        el = jnp.exp(sl - m)
        inv = 1.0 / (jnp.sum(ec, axis=-1, keepdims=True) + jnp.sum(el, axis=-1, keepdims=True))
        return ec * inv, el * inv

    p1c, p1l = probs(q1)
    p2c, p2l = probs(q2)
    lv = lv_ref[...]
    lam = (jnp.exp(jnp.sum(lv[0:1] * lv[1:2], axis=-1, keepdims=True))
           - jnp.exp(jnp.sum(lv[2:3] * lv[3:4], axis=-1, keepdims=True)) + lam_init)
    wc = (p1c - lam * p2c).astype(BF16)
    wl = (p1l - lam * p2l).astype(BF16)
    o = _dot(wc, vc_ref[...].astype(BF16)) + _dot(wl, vl_ref[...].astype(BF16))
    o_ref[...] = (_rms(o, gn_ref[...]) * (1.0 - lam_init)).astype(o_ref.dtype)


def _attention(p_lat, p_ctx, lam_vecs, norm_g, lam_init, tq):
    b, n, _ = p_lat.shape
    nc = p_ctx.shape[1]
    kb = ATT_HEADS
    vb = 2 * ATT_HEADS
    kern = functools.partial(_attn_kernel, lam_init=lam_init)
    return pl.pallas_call(
        kern,
        grid=(b, ATT_HEADS, n // tq),
        in_specs=[
            pl.BlockSpec((None, tq, LANES), lambda bi, h, qi: (bi, qi, h)),
            pl.BlockSpec((None, n, LANES), lambda bi, h, qi: (bi, 0, kb + h)),
            pl.BlockSpec((None, n, LANES), lambda bi, h, qi: (bi, 0, vb + h)),
            pl.BlockSpec((None, nc, LANES), lambda bi, h, qi: (bi, 0, kb + h)),
            pl.BlockSpec((None, nc, LANES), lambda bi, h, qi: (bi, 0, vb + h)),
            pl.BlockSpec(lam_vecs.shape, lambda bi, h, qi: (0, 0)),
            pl.BlockSpec(norm_g.shape, lambda bi, h, qi: (0, 0)),
        ],
        out_specs=pl.BlockSpec((None, tq, LANES), lambda bi, h, qi: (bi, qi, h)),
        out_shape=jax.ShapeDtypeStruct((b, n, ATT_HEADS * ATT_V), BF16),
        compiler_params=_params(("parallel", "parallel", "arbitrary"), 48),
        name="diff_attention",
    )(p_lat, p_lat, p_lat, p_ctx, p_ctx, lam_vecs, norm_g)


def _hgrn_kernel(q_ref, zf_ref, zb_ref, i_ref, g_ref, czf_ref, czb_ref, ci_ref, lb_ref, gn_ref, o_ref,
                 o_scr, qs_scr, u_scr, dl_scr, *, n_lat, n_ctx, slot, rb):
    C = REC_CHUNK
    nct = n_ctx // C
    ncl = n_lat // C
    cpb = rb // C

    def lower_bound(d):
        a = lb_ref[d]
        ex = jnp.exp(a - jnp.max(a, axis=0, keepdims=True))
        return jnp.sum(ex[:slot + 1], axis=0, keepdims=True) / jnp.sum(ex, axis=0, keepdims=True)

    r_i = lax.broadcasted_iota(jnp.int32, (rb, rb), 0)
    c_i = lax.broadcasted_iota(jnp.int32, (rb, rb), 1)
    same = (r_i // C) == (c_i // C)
    masks = (same & (c_i <= r_i), same & (c_i >= r_i))
    o_scr[...] = jnp.zeros_like(o_scr)

    def prep(d, z_ref, v_ref, base, with_out, blk):
        r0 = pl.multiple_of(blk * rb, rb)
        rows = pl.ds(r0, rb)
        mask = masks[d]
        f = lb[d] + (1.0 - lb[d]) * jax.nn.sigmoid(z_ref[rows, :])
        kk = 1.0 - f
        lg = jnp.log(f)
        hi = lg.astype(BF16)
        lo = (lg - hi.astype(F32)).astype(BF16)
        tri = mask.astype(BF16)
        cum = _dot(tri, hi) + _dot(tri, lo)
        ref_rows, last_rows = [], []
        for c in range(cpb):
            mid = c * C + (C // 2 - 1 if d == 0 else C // 2)
            end = c * C + (C - 1 if d == 0 else 0)
            ref_rows.append(jnp.broadcast_to(cum[mid:mid + 1, :], (C, LANES)))
            last_rows.append(jnp.broadcast_to(cum[end:end + 1, :], (C, LANES)))
        ref = jnp.concatenate(ref_rows, axis=0)
        last = jnp.concatenate(last_rows, axis=0)
        v = v_ref[rows, :].astype(BF16)
        kd = (kk * jnp.exp(last - cum)).astype(BF16)
        for c in range(cpb):
            idx = base + blk * cpb + c
            cs = slice(c * C, (c + 1) * C)
            u_scr[d, idx] = _dot_tn(v[cs], kd[cs])
            dl_scr[d, idx] = jnp.exp(last[c * C:c * C + 8, :])
        if with_out:
            dlt = cum - ref
            q = q_ref[rows, :] * (REC_DK ** -0.5)
            qe = q * jnp.exp(dlt)
            ke = kk * jnp.exp(-dlt)
            qs_scr[d, rows, :] = (qe * jnp.exp(ref)).astype(BF16)
            att = jnp.where(mask, _dot_nt(qe.astype(BF16), ke.astype(BF16)), 0.0).astype(BF16)
            o_scr[rows, :] += _dot(att, v)

    lb = (lower_bound(0), lower_bound(1))
    for d, cz_ref, z_ref in ((0, czf_ref, zf_ref), (1, czb_ref, zb_ref)):
        for blk in range(n_ctx // rb):
            prep(d, cz_ref, ci_ref, 0, False, blk)

        def lat_block(blk, carry, d=d, z_ref=z_ref):
            prep(d, z_ref, i_ref, nct, True, blk)
            return carry

        lax.fori_loop(0, n_lat // rb, lat_block, 0)

    def advance(d, idx, st):
        return st * dl_scr[d, idx, 0:1, :] + u_scr[d, idx]

    st_f = jnp.zeros((REC_DV, REC_DK), F32)
    st_b = jnp.zeros((REC_DV, REC_DK), F32)
    for c in range(nct):
        st_f = advance(0, c, st_f)
        st_b = advance(1, nct - 1 - c, st_b)

    def seq_step(s, carry):
        st_f, st_b = carry
        cf = s
        cb = ncl - 1 - s
        rows_f = pl.ds(pl.multiple_of(cf * C, C), C)
        rows_b = pl.ds(pl.multiple_of(cb * C, C), C)
        o_scr[rows_f, :] += _dot_nt(qs_scr[0, rows_f, :], st_f.astype(BF16))
        o_scr[rows_b, :] += _dot_nt(qs_scr[1, rows_b, :], st_b.astype(BF16))
        return advance(0, nct + cf, st_f), advance(1, nct + cb, st_b)

    lax.fori_loop(0, ncl, seq_step, (st_f, st_b))

    g = g_ref[...]
    o_ref[...] = (_rms(o_scr[...], gn_ref[...]) * (g * jax.nn.sigmoid(g))).astype(o_ref.dtype)


def _hgrn(p_lat, p_ctx, rec_lb, norm_g, slot):
    b, n, _ = p_lat.shape
    nc = p_ctx.shape[1]
    rb = math.gcd(256, math.gcd(n, nc))
    nchunks = (n + nc) // REC_CHUNK
    h0 = 3 * ATT_HEADS
    cols = [h0 + k * REC_HEADS for k in range(5)]
    lat_spec = lambda c: pl.BlockSpec((None, n, LANES), lambda bi, h: (bi, 0, c + h))
    ctx_spec = lambda c: pl.BlockSpec((None, nc, LANES), lambda bi, h: (bi, 0, c + h))
    kern = functools.partial(_hgrn_kernel, n_lat=n, n_ctx=nc, slot=slot, rb=rb)
    return pl.pallas_call(
        kern,
        grid=(b, REC_HEADS),
        in_specs=[lat_spec(cols[0]), lat_spec(cols[1]), lat_spec(cols[2]), lat_spec(cols[3]), lat_spec(cols[4]),
                  ctx_spec(cols[1]), ctx_spec(cols[2]), ctx_spec(cols[3]),
                  pl.BlockSpec((2, rec_lb.shape[1], LANES), lambda bi, h: (0, 0, h)),
                  pl.BlockSpec(norm_g.shape, lambda bi, h: (0, 0))],
        out_specs=pl.BlockSpec((None, n, LANES), lambda bi, h: (bi, 0, h)),
        out_shape=jax.ShapeDtypeStruct((b, n, REC_HEADS * REC_DV), BF16),
        scratch_shapes=[
            pltpu.VMEM((n, LANES), F32),
            pltpu.VMEM((2, n, LANES), BF16),
            pltpu.VMEM((2, nchunks, REC_DV, REC_DK), F32),
            pltpu.VMEM((2, nchunks, 8, LANES), F32),
        ],
        compiler_params=_params(("parallel", "parallel"), 48),
        name="hgrn2_bidir",
    )(p_lat, p_lat, p_lat, p_lat, p_lat, p_ctx, p_ctx, p_ctx, rec_lb, norm_g)


def _mix_out_kernel(a_ref, r_ref, w_ref, x_ref, mod_ref, g_ref, o_ref, *, k0, gi):
    ka = a_ref.shape[-1]
    y = _dot(a_ref[...], w_ref[:ka, :]) + _dot(r_ref[...], w_ref[ka:, :])
    o_ref[...] = x_ref[...] + mod_ref[k0 + 2:k0 + 3, :] * _rms(y, g_ref[gi + 1:gi + 2, :])


def _mix_out(oa, orec, w_bf, x, mod3, row_fn, g_l, k0, gi, tm):
    t, d = x.shape
    ka, kr = oa.shape[-1], orec.shape[-1]
    kern = functools.partial(_mix_out_kernel, k0=k0, gi=gi)
    return pl.pallas_call(
        kern,
        grid=(t // tm,),
        in_specs=[
            pl.BlockSpec((tm, ka), lambda i: (i, 0)),
            pl.BlockSpec((tm, kr), lambda i: (i, 0)),
            pl.BlockSpec(w_bf.shape, lambda i: (0, 0)),
            pl.BlockSpec((tm, d), lambda i: (i, 0)),
            pl.BlockSpec((None, N_MOD, d), lambda i: (row_fn(i), 0, 0)),
            pl.BlockSpec(g_l.shape, lambda i: (0, 0)),
        ],
        out_specs=pl.BlockSpec((tm, d), lambda i: (i, 0)),
        out_shape=jax.ShapeDtypeStruct((t, d), F32),
        compiler_params=_params(("parallel",), 48),
        name="mixer_out",
    )(oa, orec, w_bf, x, mod3, g_l)


def _conv_in_kernel(x_ref, mod_ref, g_ref, wb_ref, wc_ref, wv_ref, b_ref, cv_ref, h_ref, *, k0, gi):
    j = pl.program_id(1)

    @pl.when(j == 0)
    def _():
        h = _rms(x_ref[...], g_ref[gi:gi + 1, :]) * (1.0 + mod_ref[k0 + 1:k0 + 2, :]) + mod_ref[k0:k0 + 1, :]
        h_ref[...] = h.astype(BF16)

    h = h_ref[...]
    b_ref[...] = _dot(h, wb_ref[...].astype(BF16))
    cv_ref[...] = _dot(h, wc_ref[...].astype(BF16)) * _dot(h, wv_ref[...].astype(BF16))


def _conv_in(x, mod3, row_fn, g_l, w, w_idx, k0, gi, tm, tn):
    t, d = x.shape
    width = w.shape[-1] // 3
    nb = width // tn
    kern = functools.partial(_conv_in_kernel, k0=k0, gi=gi)
    w_spec = lambda off: pl.BlockSpec((None, d, tn), lambda i, j: (w_idx, 0, off + j))
    return pl.pallas_call(
        kern,
        grid=(t // tm, nb),
        in_specs=[
            pl.BlockSpec((tm, d), lambda i, j: (i, 0)),
            pl.BlockSpec((None, N_MOD, d), lambda i, j: (row_fn(i), 0, 0)),
            pl.BlockSpec(g_l.shape, lambda i, j: (0, 0)),
            w_spec(0), w_spec(nb), w_spec(2 * nb),
        ],
        out_specs=[pl.BlockSpec((tm, tn), lambda i, j: (i, j))] * 2,
        out_shape=[jax.ShapeDtypeStruct((t, width), F32)] * 2,
        scratch_shapes=[pltpu.VMEM((tm, d), BF16)],
        compiler_params=_params(("parallel", "arbitrary"), 56),
        name="conv_in",
    )(x, mod3, g_l, w, w, w)


def _conv_out_kernel(b_ref, cv_ref, prev_ref, next_ref, cw_ref, w_ref, x_ref, mod_ref, g_ref, o_ref,
                     *, k0, gi, seq_tiles):
    i = pl.program_id(0)
    tm = cv_ref.shape[0]
    cv = cv_ref[...]
    row = lax.broadcasted_iota(jnp.int32, (tm, 1), 0)
    prev = jnp.where(i % seq_tiles == 0, 0.0, prev_ref[7:8, :])
    nxt = jnp.where(i % seq_tiles == seq_tiles - 1, 0.0, next_ref[0:1, :])
    up = jnp.where(row == 0, prev, pltpu.roll(cv, 1, axis=0))
    dn = jnp.where(row == tm - 1, nxt, pltpu.roll(cv, tm - 1, axis=0))
    u = cw_ref[0:1, :] * up + cw_ref[1:2, :] * cv + cw_ref[2:3, :] * dn
    y = _dot((b_ref[...] * u).astype(BF16), w_ref[...])
    o_ref[...] = x_ref[...] + mod_ref[k0 + 2:k0 + 3, :] * _rms(y, g_ref[gi + 1:gi + 2, :])


def _conv_out(bg, cv, conv_w, w_bf, x, mod3, row_fn, g_l, k0, gi, tm, seq):
    t, d = x.shape
    width = cv.shape[-1]
    halo = 8
    hb = tm // halo
    last = t // halo - 1
    kern = functools.partial(_conv_out_kernel, k0=k0, gi=gi, seq_tiles=seq // tm)
    return pl.pallas_call(
        kern,
        grid=(t // tm,),
        in_specs=[
            pl.BlockSpec((tm, width), lambda i: (i, 0)),
            pl.BlockSpec((tm, width), lambda i: (i, 0)),
            pl.BlockSpec((halo, width), lambda i: (jnp.maximum(i * hb - 1, 0), 0)),
            pl.BlockSpec((halo, width), lambda i: (jnp.minimum((i + 1) * hb, last), 0)),
            pl.BlockSpec(conv_w.shape, lambda i: (0, 0)),
            pl.BlockSpec(w_bf.shape, lambda i: (0, 0)),
            pl.BlockSpec((tm, d), lambda i: (i, 0)),
            pl.BlockSpec((None, N_MOD, d), lambda i: (row_fn(i), 0, 0)),
            pl.BlockSpec(g_l.shape, lambda i: (0, 0)),
        ],
        out_specs=pl.BlockSpec((tm, d), lambda i: (i, 0)),
        out_shape=jax.ShapeDtypeStruct((t, d), F32),
        compiler_params=_params(("parallel",), 48),
        name="conv_out",
    )(bg, cv, cv, cv, conv_w, w_bf, x, mod3, g_l)


def _tile(n, pref):
    t = pref
    while n % t:
        t //= 2
    return t


def kernel(x, c, ctx, c_ctx, ada_w, ada_b, norm_g, ffn_w_gate, ffn_w_up, ffn_w_down, mix_w_in, mix_w_out,
           diff_lambda, diff_norm_g, rec_norm_g, rec_lb, conv_w_in, conv_w, conv_w_out):
    batch, seq, d = x.shape
    ctx_len = ctx.shape[1]
    depth = ada_w.shape[0]
    ctx_row = batch
    assert batch < MOD_ROWS

    cvec = jnp.concatenate([c, c_ctx[None, :], jnp.zeros((MOD_ROWS - batch - 1, d), F32)], axis=0)
    xl = x.reshape(batch * seq, d)
    xc = ctx.reshape(batch * ctx_len, d)

    tm_l = _tile(seq, 1024)
    tm_c = _tile(batch * ctx_len, 1024)
    lat_row = lambda tm: (lambda i: i // (seq // tm))
    ctx_rowf = lambda i: ctx_row

    for l in range(depth):
        even = l % 2 == 0
        ctx_out = any(j % 2 == 0 for j in range(l + 1, depth))
        ctx_in = even or ctx_out
        g_l = norm_g[l]
        mod3 = _modulation(cvec, ada_w, ada_b, l)
        ffn = lambda xx, row_fn, half, k0, gi, tm: _ffn_half(
            xx, mod3, row_fn, g_l, ffn_w_gate, ffn_w_up, ffn_w_down, l, half, k0, gi, tm)

        xl = ffn(xl, lat_row(tm_l), 0, 0, 0, tm_l)
        if ctx_in:
            xc = ffn(xc, ctx_rowf, 0, 0, 0, tm_c)

        tm_o = _tile(seq, 512)
        if even:
            e = l // 2
            lam_init = LAMBDA_INIT_BASE - LAMBDA_INIT_AMP * math.exp(-LAMBDA_INIT_RATE * l)
            qk_cols = 2 * ATT_HEADS * 2 * ATT_QK
            tn = 512
            rope = _rope_tables(seq, tn // ATT_QK) + (qk_cols,)
            p_lat = _proj(xl, mod3, lat_row(tm_l), g_l, mix_w_in, e, 3, 2, tm_l, tn, rope=rope)
            p_ctx = _proj(xc, mod3, ctx_rowf, g_l, mix_w_in, e, 3, 2, tm_c, tn)
            p_lat = p_lat.reshape(batch, seq, -1)
            p_ctx = p_ctx.reshape(batch, ctx_len, -1)
            oa = _attention(p_lat, p_ctx, diff_lambda[e], diff_norm_g[e][None, :], lam_init, _tile(seq, 256))
            orec = _hgrn(p_lat, p_ctx, rec_lb, rec_norm_g[e][None, :], e)
            w_bf = mix_w_out[e].astype(BF16)
            xl_new = _mix_out(oa.reshape(batch * seq, -1), orec.reshape(batch * seq, -1), w_bf, xl, mod3,
                              lat_row(tm_o), g_l, 3, 2, tm_o)
            if ctx_out:
                raise NotImplementedError("context mixer output is not needed for this depth")
        else:
            o = l // 2
            bg, cv = _conv_in(xl, mod3, lat_row(tm_l), g_l, conv_w_in, o, 3, 2, tm_l, 256)
            xl_new = _conv_out(bg, cv, conv_w[o], conv_w_out[o].astype(BF16), xl, mod3, lat_row(tm_o), g_l,
                               3, 2, tm_o, seq)
            if ctx_out:
                raise NotImplementedError("context mixer output is not needed for this depth")
        xl = xl_new
        xl = ffn(xl, lat_row(tm_l), 1, 6, 4, tm_l)
    return xl.reshape(batch, seq, d)
```

```python
import functools
import math

import jax
import jax.numpy as jnp
from jax import lax
from jax.experimental import pallas as pl
from jax.experimental.pallas import tpu as pltpu

F32 = jnp.float32
BF16 = jnp.bfloat16

EPS = 1e-6
N_MOD = 9
FFN_RES_WEIGHT = 0.5
GRID_W = 64
ATT_HEADS = 8
ATT_QK = 64
ATT_V = 2 * ATT_QK
ROPE_THETA = 10000.0
ROPE_PAIRS = ATT_QK // 4
LAMBDA_INIT_BASE = 0.8
LAMBDA_INIT_AMP = 0.6
LAMBDA_INIT_RATE = 0.3
REC_HEADS = 8
REC_DK = 128
REC_DV = 128
REC_CHUNK = 64
CONV_K = 3

LANES = 128
MIB = 1024 * 1024
MOD_ROWS = 8
LOG2E = 1.4426950408889634


def _params(sem, vmem_mib):
    return pltpu.CompilerParams(dimension_semantics=sem, vmem_limit_bytes=vmem_mib * MIB)


def _rms(y, g):
    return y * lax.rsqrt(jnp.mean(y * y, axis=-1, keepdims=True) + EPS) * g


def _dot(a, b):
    return jnp.dot(a, b, preferred_element_type=F32)


def _dot_nt(a, b):
    return lax.dot_general(a, b, (((1,), (1,)), ((), ())), preferred_element_type=F32)


def _dot_tn(a, b):
    return lax.dot_general(a, b, (((0,), (0,)), ((), ())), preferred_element_type=F32)


def _mod_kernel(c_ref, w_ref, b_ref, o_ref):
    cv = c_ref[...]
    s = (cv * jax.nn.sigmoid(cv)).astype(BF16)
    o_ref[...] = _dot(s, w_ref[...].astype(BF16)) + b_ref[...]


def _modulation(cvec, ada_w, ada_b, layer):
    depth, d, n = ada_w.shape
    tn = 1024
    b3 = ada_b.reshape(depth, 1, n)
    out = pl.pallas_call(
        _mod_kernel,
        grid=(n // tn,),
        in_specs=[
            pl.BlockSpec((MOD_ROWS, d), lambda j: (0, 0)),
            pl.BlockSpec((None, d, tn), lambda j: (layer, 0, j)),
            pl.BlockSpec((None, 1, tn), lambda j: (layer, 0, j)),
        ],
        out_specs=pl.BlockSpec((MOD_ROWS, tn), lambda j: (0, j)),
        out_shape=jax.ShapeDtypeStruct((MOD_ROWS, n), F32),
        compiler_params=_params(("arbitrary",), 40),
        name="modulation",
    )(cvec, ada_w, b3)
    return out.reshape(MOD_ROWS, N_MOD, d)


def _ffn_kernel(x_ref, mod_ref, g_ref, wg_ref, wu_ref, wd_ref, o_ref, h_ref, *, k0, gi, nj, tf, tail, rc):
    j = pl.program_id(1)
    row_chunks = [slice(r, r + rc) for r in range(0, x_ref.shape[0], rc)]

    def weights(width):
        return (wg_ref[:, :width].astype(BF16), wu_ref[:, :width].astype(BF16), wd_ref[:width, :].astype(BF16))

    def partial_down(h, w):
        g = _dot(h, w[0])
        u = _dot(h, w[1])
        a = (g * jax.nn.sigmoid(g) * u).astype(BF16)
        return _dot(a, w[2])

    @pl.when(j == 0)
    def _():
        w = weights(tf)
        for rows in row_chunks:
            h = _rms(x_ref[rows, :], g_ref[gi:gi + 1, :]) * (1.0 + mod_ref[k0 + 1:k0 + 2, :]) + mod_ref[k0:k0 + 1, :]
            h = h.astype(BF16)
            h_ref[rows, :] = h
            o_ref[rows, :] = partial_down(h, w)

    @pl.when((j > 0) & (j < nj - 1))
    def _():
        o_ref[...] += partial_down(h_ref[...], weights(tf))

    @pl.when(j == nj - 1)
    def _():
        w = weights(tail)
        gate = FFN_RES_WEIGHT * mod_ref[k0 + 2:k0 + 3, :]
        for rows in row_chunks:
            y = o_ref[rows, :] + partial_down(h_ref[rows, :], w)
            o_ref[rows, :] = x_ref[rows, :] + gate * _rms(y, g_ref[gi + 1:gi + 2, :])


def _ffn_half(x, mod3, row_fn, g_l, wg, wu, wd, layer, half, k0, gi, tm, tf=256):
    t, d = x.shape
    f = wg.shape[-1]
    nj = pl.cdiv(f, tf)
    tail = f - (nj - 1) * tf
    kern = functools.partial(_ffn_kernel, k0=k0, gi=gi, nj=nj, tf=tf, tail=tail, rc=min(tm, 256))
    return pl.pallas_call(
        kern,
        grid=(t // tm, nj),
        in_specs=[
            pl.BlockSpec((tm, d), lambda i, j: (i, 0), pipeline_mode=pl.Buffered(1)),
            pl.BlockSpec((None, N_MOD, d), lambda i, j: (row_fn(i), 0, 0)),
            pl.BlockSpec(g_l.shape, lambda i, j: (0, 0)),
            pl.BlockSpec((None, None, d, tf), lambda i, j: (layer, half, 0, j)),
            pl.BlockSpec((None, None, d, tf), lambda i, j: (layer, half, 0, j)),
            pl.BlockSpec((None, None, tf, d), lambda i, j: (layer, half, j, 0)),
        ],
        out_specs=pl.BlockSpec((tm, d), lambda i, j: (i, 0)),
        out_shape=jax.ShapeDtypeStruct((t, d), F32),
        scratch_shapes=[pltpu.VMEM((tm, d), BF16)],
        compiler_params=_params(("parallel", "arbitrary"), 56),
        name="ffn_half",
    )(x, mod3, g_l, wg, wu, wd)


def _proj_kernel(*refs, k0, gi, n_lo, rope_blocks):
    if rope_blocks:
        x_ref, mod_ref, g_ref, w_ref, cos_ref, sin_ref, lo_ref, hi_ref, h_ref = refs
    else:
        x_ref, mod_ref, g_ref, w_ref, lo_ref, hi_ref, h_ref = refs
    j = pl.program_id(1)

    @pl.when(j == 0)
    def _():
        h = _rms(x_ref[...], g_ref[gi:gi + 1, :]) * (1.0 + mod_ref[k0 + 1:k0 + 2, :]) + mod_ref[k0:k0 + 1, :]
        h_ref[...] = h.astype(BF16)

    p = _dot(h_ref[...], w_ref[...].astype(BF16))

    @pl.when(j >= n_lo)
    def _():
        hi_ref[...] = p

    if rope_blocks:
        @pl.when(j < rope_blocks)
        def _():
            tn = p.shape[-1]
            lane = lax.broadcasted_iota(jnp.int32, (1, tn), 1)
            first = (lane % (2 * ROPE_PAIRS)) < ROPE_PAIRS
            partner = jnp.where(first, pltpu.roll(p, tn - ROPE_PAIRS, axis=1), pltpu.roll(p, ROPE_PAIRS, axis=1))
            lo_ref[...] = (p * cos_ref[...] + partner * sin_ref[...]).astype(BF16)

    @pl.when((j >= rope_blocks) & (j < n_lo))
    def _():
        lo_ref[...] = p.astype(BF16)


def _proj(x, mod3, row_fn, g_l, w, w_idx, k0, gi, tm, tn, lo_cols, rope=None):
    t, d = x.shape
    n = w.shape[-1]
    n_lo = lo_cols // tn
    in_specs = [
        pl.BlockSpec((tm, d), lambda i, j: (i, 0)),
        pl.BlockSpec((None, N_MOD, d), lambda i, j: (row_fn(i), 0, 0)),
        pl.BlockSpec(g_l.shape, lambda i, j: (0, 0)),
        pl.BlockSpec((None, d, tn), lambda i, j: (w_idx, 0, j)),
    ]
    args = [x, mod3, g_l, w]
    rope_blocks = 0
    if rope is not None:
        cos_t, sin_t, q_cols, k_cols = rope
        seq_tiles = cos_t.shape[1] // tm
        q_blocks = q_cols // tn
        rope_blocks = (q_cols + k_cols) // tn
        in_specs += [pl.BlockSpec((None, tm, tn), lambda i, j: (jnp.minimum(j // q_blocks, 1), i % seq_tiles, 0))] * 2
        args += [cos_t, sin_t]
    kern = functools.partial(_proj_kernel, k0=k0, gi=gi, n_lo=n_lo, rope_blocks=rope_blocks)
    return pl.pallas_call(
        kern,
        grid=(t // tm, n // tn),
        in_specs=in_specs,
        out_specs=[pl.BlockSpec((tm, tn), lambda i, j: (i, jnp.minimum(j, n_lo - 1))),
                   pl.BlockSpec((tm, tn), lambda i, j: (i, jnp.maximum(j - n_lo, 0)))],
        out_shape=[jax.ShapeDtypeStruct((t, lo_cols), BF16), jax.ShapeDtypeStruct((t, n - lo_cols), F32)],
        scratch_shapes=[pltpu.VMEM((tm, d), BF16)],
        compiler_params=_params(("parallel", "arbitrary"), 56),
        name="norm_proj",
    )(*args)


def _rope_tables(seq, reps, q_scale):
    n = jnp.arange(seq)
    row = (n // GRID_W).astype(F32)
    col = (n % GRID_W).astype(F32)
    freqs = ROPE_THETA ** (-jnp.arange(ROPE_PAIRS, dtype=F32) / ROPE_PAIRS)

    def half(pos):
        ang = pos[:, None] * freqs
        c, s = jnp.cos(ang), jnp.sin(ang)
        return jnp.concatenate([c, c], axis=-1), jnp.concatenate([-s, s], axis=-1)

    cr, sr = half(row)
    cc, sc = half(col)
    cos_t = jnp.tile(jnp.concatenate([cr, cc], axis=-1), (1, reps))
    sin_t = jnp.tile(jnp.concatenate([sr, sc], axis=-1), (1, reps))
    return jnp.stack([cos_t * q_scale, cos_t]), jnp.stack([sin_t * q_scale, sin_t])


def _attn_kernel(q_ref, kl_ref, vl_ref, kc_ref, vc_ref, lv_ref, gn_ref, o_ref,
                 k_scr, v_scr, s0_scr, s1_scr, w0_scr, w1_scr, *, lam_init, tq, rc):
    n = q_ref.shape[0]
    nc = kc_ref.shape[0]
    nt = n // tq
    assert nt % 2 == 0 and nt >= 4
    k_scr[:nc, :] = kc_ref[...]
    k_scr[nc:, :] = kl_ref[...]
    v_scr[:nc, :] = vc_ref[...]
    v_scr[nc:, :] = vl_ref[...]
    lane = lax.broadcasted_iota(jnp.int32, (1, 2 * ATT_QK), 1)
    lv = lv_ref[...]
    lam = (jnp.exp(jnp.sum(lv[0:1] * lv[1:2], axis=-1, keepdims=True))
           - jnp.exp(jnp.sum(lv[2:3] * lv[3:4], axis=-1, keepdims=True)) + lam_init)
    s_scr = (s0_scr, s1_scr)
    w_scr = (w0_scr, w1_scr)

    def tile_rows(t):
        return pl.ds(pl.multiple_of(t * tq, tq), tq)

    def scores(t, slot):
        q = q_ref[tile_rows(t), :]
        zero = jnp.zeros_like(q)
        k = k_scr[...]
        s_scr[slot][0] = _dot_nt(jnp.where(lane < ATT_QK, q, zero), k)
        s_scr[slot][1] = _dot_nt(jnp.where(lane >= ATT_QK, q, zero), k)

    def weights(slot):
        for r in range(0, tq, rc):
            def numerators(m):
                s = s_scr[slot][m, r:r + rc, :]
                e = jnp.exp2(s - jnp.max(s, axis=-1, keepdims=True))
                return e, 1.0 / jnp.sum(e, axis=-1, keepdims=True)

            e1, inv1 = numerators(0)
            e2, inv2 = numerators(1)
            w_scr[slot][r:r + rc, :] = (e1 * inv1 - e2 * (lam * inv2)).astype(BF16)

    def outputs(t, slot):
        o = _dot(w_scr[slot][...], v_scr[...])
        o_ref[tile_rows(t), :] = (_rms(o, gn_ref[...]) * (1.0 - lam_init)).astype(o_ref.dtype)

    scores(0, 0)
    scores(1, 1)
    weights(0)

    def steady(p, carry):
        t = 2 * p
        scores(t, 0)
        weights(1)
        outputs(t - 2, 0)
        scores(t + 1, 1)
        weights(0)
        outputs(t - 1, 1)
        return carry

    lax.fori_loop(1, nt // 2, steady, 0)
    weights(1)
    outputs(nt - 2, 0)
    outputs(nt - 1, 1)


def _attention(a_lat, a_ctx, lam_vecs, norm_g, lam_init, tq):
    b, n, _ = a_lat.shape
    nc = a_ctx.shape[1]
    kb = ATT_HEADS
    vb = 2 * ATT_HEADS
    kern = functools.partial(_attn_kernel, lam_init=lam_init, tq=tq, rc=min(tq, 256))
    nk = n + nc
    return pl.pallas_call(
        kern,
        grid=(b, ATT_HEADS),
        in_specs=[
            pl.BlockSpec((None, n, LANES), lambda bi, h: (bi, 0, h)),
            pl.BlockSpec((None, n, LANES), lambda bi, h: (bi, 0, kb + h)),
            pl.BlockSpec((None, n, LANES), lambda bi, h: (bi, 0, vb + h)),
            pl.BlockSpec((None, nc, LANES), lambda bi, h: (bi, 0, kb + h)),
            pl.BlockSpec((None, nc, LANES), lambda bi, h: (bi, 0, vb + h)),
            pl.BlockSpec(lam_vecs.shape, lambda bi, h: (0, 0)),
            pl.BlockSpec(norm_g.shape, lambda bi, h: (0, 0)),
        ],
        out_specs=pl.BlockSpec((None, n, LANES), lambda bi, h: (bi, 0, h)),
        out_shape=jax.ShapeDtypeStruct((b, n, ATT_HEADS * ATT_V), BF16),
        scratch_shapes=[
            pltpu.VMEM((nk, LANES), BF16),
            pltpu.VMEM((nk, LANES), BF16),
            pltpu.VMEM((2, tq, nk), F32),
            pltpu.VMEM((2, tq, nk), F32),
            pltpu.VMEM((tq, nk), BF16),
            pltpu.VMEM((tq, nk), BF16),
        ],
        compiler_params=_params(("parallel", "parallel"), 48),
        name="diff_attention",
    )(a_lat, a_lat, a_lat, a_ctx, a_ctx, lam_vecs, norm_g)


def _hgrn_kernel(q_ref, zf_ref, zb_ref, i_ref, g_ref, czf_ref, czb_ref, ci_ref, lb_ref, gn_ref, o_ref,
                 o_scr, qs_scr, u_scr, dl_scr, *, n_lat, n_ctx, slot, rb):
    C = REC_CHUNK
    nct = n_ctx // C
    ncl = n_lat // C
    cpb = rb // C

    def lower_bound(d):
        a = lb_ref[d]
        ex = jnp.exp(a - jnp.max(a, axis=0, keepdims=True))
        return jnp.sum(ex[:slot + 1], axis=0, keepdims=True) / jnp.sum(ex, axis=0, keepdims=True)

    lb = (lower_bound(0), lower_bound(1))
    r_i = lax.broadcasted_iota(jnp.int32, (rb, rb), 0)
    c_i = lax.broadcasted_iota(jnp.int32, (rb, rb), 1)
    same = (r_i // C) == (c_i // C)
    masks = (same & (c_i <= r_i), same & (c_i >= r_i))

    def prep(d, z_ref, v, q, base, blk):
        rows = pl.ds(pl.multiple_of(blk * rb, rb), rb)
        mask = masks[d]
        f = lb[d] + (1.0 - lb[d]) * jax.nn.sigmoid(z_ref[rows, :])
        kk = 1.0 - f
        lg = jnp.log(f)
        hi = lg.astype(BF16)
        lo = (lg - hi.astype(F32)).astype(BF16)
        tri = mask.astype(BF16)
        cum = _dot(tri, hi) + _dot(tri, lo)
        ref_rows, last_rows = [], []
        for c in range(cpb):
            mid = c * C + (C // 2 - 1 if d == 0 else C // 2)
            end = c * C + (C - 1 if d == 0 else 0)
            ref_rows.append(jnp.broadcast_to(cum[mid:mid + 1, :], (C, LANES)))
            last_rows.append(jnp.broadcast_to(cum[end:end + 1, :], (C, LANES)))
        ref = jnp.concatenate(ref_rows, axis=0)
        last = jnp.concatenate(last_rows, axis=0)
        kd = (kk * jnp.exp(last - cum)).astype(BF16)
        for c in range(cpb):
            idx = base + blk * cpb + c
            cs = slice(c * C, (c + 1) * C)
            u_scr[d, idx] = _dot_tn(v[cs], kd[cs])
            dl_scr[d, idx] = jnp.exp(last[c * C:c * C + 8, :])
        if q is None:
            return None
        dlt = cum - ref
        qe = q * jnp.exp(dlt)
        ke = kk * jnp.exp(-dlt)
        qs_scr[d, rows, :] = (qe * jnp.exp(ref)).astype(BF16)
        att = jnp.where(mask, _dot_nt(qe.astype(BF16), ke.astype(BF16)), 0.0).astype(BF16)
        return _dot(att, v)

    for blk in range(n_ctx // rb):
        v = ci_ref[blk * rb:(blk + 1) * rb, :].astype(BF16)
        prep(0, czf_ref, v, None, 0, blk)
        prep(1, czb_ref, v, None, 0, blk)

    def lat_block(blk, carry):
        rows = pl.ds(pl.multiple_of(blk * rb, rb), rb)
        v = i_ref[rows, :].astype(BF16)
        q = q_ref[rows, :] * (REC_DK ** -0.5)
        o_scr[rows, :] = prep(0, zf_ref, v, q, nct, blk) + prep(1, zb_ref, v, q, nct, blk)
        return carry

    lax.fori_loop(0, n_lat // rb, lat_block, 0, unroll=2)

    def advance(d, idx, st):
        return st * dl_scr[d, idx, 0:1, :] + u_scr[d, idx]

    st_f = jnp.zeros((REC_DV, REC_DK), F32)
    st_b = jnp.zeros((REC_DV, REC_DK), F32)
    for c in range(nct):
        st_f = advance(0, c, st_f)
        st_b = advance(1, nct - 1 - c, st_b)

    def seq_step(s, carry):
        st_f, st_b = carry
        cf = s
        cb = ncl - 1 - s
        rows_f = pl.ds(pl.multiple_of(cf * C, C), C)
        rows_b = pl.ds(pl.multiple_of(cb * C, C), C)
        o_scr[rows_f, :] += _dot_nt(qs_scr[0, rows_f, :], st_f.astype(BF16))
        o_scr[rows_b, :] += _dot_nt(qs_scr[1, rows_b, :], st_b.astype(BF16))
        return advance(0, nct + cf, st_f), advance(1, nct + cb, st_b)

    lax.fori_loop(0, ncl, seq_step, (st_f, st_b), unroll=4)

    g = g_ref[...]
    o_ref[...] = (_rms(o_scr[...], gn_ref[...]) * (g * jax.nn.sigmoid(g))).astype(o_ref.dtype)


def _hgrn(r_lat, r_ctx, rec_lb, norm_g, slot):
    b, n, _ = r_lat.shape
    nc = r_ctx.shape[1]
    rb = math.gcd(256, math.gcd(n, nc))
    nchunks = (n + nc) // REC_CHUNK
    cols = [k * REC_HEADS for k in range(5)]
    lat_spec = lambda c: pl.BlockSpec((None, n, LANES), lambda bi, h: (bi, 0, c + h))
    ctx_spec = lambda c: pl.BlockSpec((None, nc, LANES), lambda bi, h: (bi, 0, c + h))
    kern = functools.partial(_hgrn_kernel, n_lat=n, n_ctx=nc, slot=slot, rb=rb)
    return pl.pallas_call(
        kern,
        grid=(b, REC_HEADS),
        in_specs=[lat_spec(cols[0]), lat_spec(cols[1]), lat_spec(cols[2]), lat_spec(cols[3]), lat_spec(cols[4]),
                  ctx_spec(cols[1]), ctx_spec(cols[2]), ctx_spec(cols[3]),
                  pl.BlockSpec((2, rec_lb.shape[1], LANES), lambda bi, h: (0, 0, h)),
                  pl.BlockSpec(norm_g.shape, lambda bi, h: (0, 0))],
        out_specs=pl.BlockSpec((None, n, LANES), lambda bi, h: (bi, 0, h)),
        out_shape=jax.ShapeDtypeStruct((b, n, REC_HEADS * REC_DV), BF16),
        scratch_shapes=[
            pltpu.VMEM((n, LANES), F32),
            pltpu.VMEM((2, n, LANES), BF16),
            pltpu.VMEM((2, nchunks, REC_DV, REC_DK), F32),
            pltpu.VMEM((2, nchunks, 8, LANES), F32),
        ],
        compiler_params=_params(("parallel", "parallel"), 48),
        name="hgrn2_bidir",
    )(r_lat, r_lat, r_lat, r_lat, r_lat, r_ctx, r_ctx, r_ctx, rec_lb, norm_g)


def _mix_out_kernel(a_ref, r_ref, w_ref, x_ref, mod_ref, g_ref, o_ref, *, k0, gi):
    ka = a_ref.shape[-1]
    y = _dot(a_ref[...], w_ref[:ka, :]) + _dot(r_ref[...], w_ref[ka:, :])
    o_ref[...] = x_ref[...] + mod_ref[k0 + 2:k0 + 3, :] * _rms(y, g_ref[gi + 1:gi + 2, :])


def _mix_out(oa, orec, w_bf, x, mod3, row_fn, g_l, k0, gi, tm):
    t, d = x.shape
    ka, kr = oa.shape[-1], orec.shape[-1]
    kern = functools.partial(_mix_out_kernel, k0=k0, gi=gi)
    return pl.pallas_call(
        kern,
        grid=(t // tm,),
        in_specs=[
            pl.BlockSpec((tm, ka), lambda i: (i, 0)),
            pl.BlockSpec((tm, kr), lambda i: (i, 0)),
            pl.BlockSpec(w_bf.shape, lambda i: (0, 0)),
            pl.BlockSpec((tm, d), lambda i: (i, 0)),
            pl.BlockSpec((None, N_MOD, d), lambda i: (row_fn(i), 0, 0)),
            pl.BlockSpec(g_l.shape, lambda i: (0, 0)),
        ],
        out_specs=pl.BlockSpec((tm, d), lambda i: (i, 0)),
        out_shape=jax.ShapeDtypeStruct((t, d), F32),
        compiler_params=_params(("parallel",), 48),
        name="mixer_out",
    )(oa, orec, w_bf, x, mod3, g_l)


def _conv_in_kernel(x_ref, mod_ref, g_ref, wb_ref, wc_ref, wv_ref, b_ref, cv_ref, h_ref, *, k0, gi):
    j = pl.program_id(1)

    @pl.when(j == 0)
    def _():
        h = _rms(x_ref[...], g_ref[gi:gi + 1, :]) * (1.0 + mod_ref[k0 + 1:k0 + 2, :]) + mod_ref[k0:k0 + 1, :]
        h_ref[...] = h.astype(BF16)

    h = h_ref[...]
    b_ref[...] = _dot(h, wb_ref[...].astype(BF16))
    cv_ref[...] = _dot(h, wc_ref[...].astype(BF16)) * _dot(h, wv_ref[...].astype(BF16))


def _conv_in(x, mod3, row_fn, g_l, w, w_idx, k0, gi, tm, tn):
    t, d = x.shape
    width = w.shape[-1] // 3
    nb = width // tn
    kern = functools.partial(_conv_in_kernel, k0=k0, gi=gi)
    w_spec = lambda off: pl.BlockSpec((None, d, tn), lambda i, j: (w_idx, 0, off + j))
    return pl.pallas_call(
        kern,
        grid=(t // tm, nb),
        in_specs=[
            pl.BlockSpec((tm, d), lambda i, j: (i, 0)),
            pl.BlockSpec((None, N_MOD, d), lambda i, j: (row_fn(i), 0, 0)),
            pl.BlockSpec(g_l.shape, lambda i, j: (0, 0)),
            w_spec(0), w_spec(nb), w_spec(2 * nb),
        ],
        out_specs=[pl.BlockSpec((tm, tn), lambda i, j: (i, j))] * 2,
        out_shape=[jax.ShapeDtypeStruct((t, width), F32)] * 2,
        scratch_shapes=[pltpu.VMEM((tm, d), BF16)],
        compiler_params=_params(("parallel", "arbitrary"), 56),
        name="conv_in",
    )(x, mod3, g_l, w, w, w)


def _conv_out_kernel(b_ref, cv_ref, prev_ref, next_ref, cw_ref, w_ref, x_ref, mod_ref, g_ref, o_ref,
                     *, k0, gi, seq_tiles):
    i = pl.program_id(0)
    tm = cv_ref.shape[0]
    cv = cv_ref[...]
    row = lax.broadcasted_iota(jnp.int32, (tm, 1), 0)
    prev = jnp.where(i % seq_tiles == 0, 0.0, prev_ref[7:8, :])
    nxt = jnp.where(i % seq_tiles == seq_tiles - 1, 0.0, next_ref[0:1, :])
    up = jnp.where(row == 0, prev, pltpu.roll(cv, 1, axis=0))
    dn = jnp.where(row == tm - 1, nxt, pltpu.roll(cv, tm - 1, axis=0))
    u = cw_ref[0:1, :] * up + cw_ref[1:2, :] * cv + cw_ref[2:3, :] * dn
    y = _dot((b_ref[...] * u).astype(BF16), w_ref[...])
    o_ref[...] = x_ref[...] + mod_ref[k0 + 2:k0 + 3, :] * _rms(y, g_ref[gi + 1:gi + 2, :])


def _conv_out(bg, cv, conv_w, w_bf, x, mod3, row_fn, g_l, k0, gi, tm, seq):
    t, d = x.shape
    width = cv.shape[-1]
    halo = 8
    hb = tm // halo
    last = t // halo - 1
    kern = functools.partial(_conv_out_kernel, k0=k0, gi=gi, seq_tiles=seq // tm)
    return pl.pallas_call(
        kern,
        grid=(t // tm,),
        in_specs=[
            pl.BlockSpec((tm, width), lambda i: (i, 0)),
            pl.BlockSpec((tm, width), lambda i: (i, 0)),
            pl.BlockSpec((halo, width), lambda i: (jnp.maximum(i * hb - 1, 0), 0)),
            pl.BlockSpec((halo, width), lambda i: (jnp.minimum((i + 1) * hb, last), 0)),
            pl.BlockSpec(conv_w.shape, lambda i: (0, 0)),
            pl.BlockSpec(w_bf.shape, lambda i: (0, 0)),
            pl.BlockSpec((tm, d), lambda i: (i, 0)),
            pl.BlockSpec((None, N_MOD, d), lambda i: (row_fn(i), 0, 0)),
            pl.BlockSpec(g_l.shape, lambda i: (0, 0)),
        ],
        out_specs=pl.BlockSpec((tm, d), lambda i: (i, 0)),
        out_shape=jax.ShapeDtypeStruct((t, d), F32),
        compiler_params=_params(("parallel",), 48),
        name="conv_out",
    )(bg, cv, cv, cv, conv_w, w_bf, x, mod3, g_l)


def _tile(n, pref):
    t = pref
    while n % t:
        t //= 2
    return t


def kernel(x, c, ctx, c_ctx, ada_w, ada_b, norm_g, ffn_w_gate, ffn_w_up, ffn_w_down, mix_w_in, mix_w_out,
           diff_lambda, diff_norm_g, rec_norm_g, rec_lb, conv_w_in, conv_w, conv_w_out):
    batch, seq, d = x.shape
    ctx_len = ctx.shape[1]
    depth = ada_w.shape[0]
    ctx_row = batch
    assert batch < MOD_ROWS

    cvec = jnp.concatenate([c, c_ctx[None, :], jnp.zeros((MOD_ROWS - batch - 1, d), F32)], axis=0)
    xl = x.reshape(batch * seq, d)
    xc = ctx.reshape(batch * ctx_len, d)

    tm_l = _tile(seq, 1024)
    tm_c = _tile(batch * ctx_len, 1024)
    lat_row = lambda tm: (lambda i: i // (seq // tm))
    ctx_rowf = lambda i: ctx_row

    for l in range(depth):
        even = l % 2 == 0
        ctx_out = any(j % 2 == 0 for j in range(l + 1, depth))
        ctx_in = even or ctx_out
        g_l = norm_g[l]
        mod3 = _modulation(cvec, ada_w, ada_b, l)
        ffn = lambda xx, row_fn, half, k0, gi, tm: _ffn_half(
            xx, mod3, row_fn, g_l, ffn_w_gate, ffn_w_up, ffn_w_down, l, half, k0, gi, tm)

        xl = ffn(xl, lat_row(tm_l), 0, 0, 0, tm_l)
        if ctx_in:
            xc = ffn(xc, ctx_rowf, 0, 0, 0, tm_c)

        tm_o = _tile(seq, 512)
        if even:
            e = l // 2
            lam_init = LAMBDA_INIT_BASE - LAMBDA_INIT_AMP * math.exp(-LAMBDA_INIT_RATE * l)
            q_cols = k_cols = ATT_HEADS * 2 * ATT_QK
            att_cols = q_cols + k_cols + ATT_HEADS * ATT_V
            tn = 512
            rope = _rope_tables(seq, tn // ATT_QK, ATT_QK ** -0.5 * LOG2E) + (q_cols, k_cols)
            a_lat, r_lat = _proj(xl, mod3, lat_row(tm_l), g_l, mix_w_in, e, 3, 2, tm_l, tn, att_cols, rope=rope)
            a_ctx, r_ctx = _proj(xc, mod3, ctx_rowf, g_l, mix_w_in, e, 3, 2, tm_c, tn, att_cols)
            lat3 = lambda a: a.reshape(batch, seq, -1)
            ctx3 = lambda a: a.reshape(batch, ctx_len, -1)
            oa = _attention(lat3(a_lat), ctx3(a_ctx), diff_lambda[e], diff_norm_g[e][None, :], lam_init,
                            min(256, seq // 4))
            orec = _hgrn(lat3(r_lat), ctx3(r_ctx), rec_lb, rec_norm_g[e][None, :], e)
            w_bf = mix_w_out[e].astype(BF16)
            xl_new = _mix_out(oa.reshape(batch * seq, -1), orec.reshape(batch * seq, -1), w_bf, xl, mod3,
                              lat_row(tm_o), g_l, 3, 2, tm_o)
            if ctx_out:
                raise NotImplementedError("context mixer output is not needed for this depth")
        else:
            o = l // 2
            bg, cv = _conv_in(xl, mod3, lat_row(tm_l), g_l, conv_w_in, o, 3, 2, tm_l, 256)
            xl_new = _conv_out(bg, cv, conv_w[o], conv_w_out[o].astype(BF16), xl, mod3, lat_row(tm_o), g_l,
                               3, 2, tm_o, seq)
            if ctx_out:
                raise NotImplementedError("context mixer output is not needed for this depth")
        xl = xl_new
        xl = ffn(xl, lat_row(tm_l), 1, 6, 4, tm_l)
    return xl.reshape(batch, seq, d)
```

```python
import functools
import math

import jax
import jax.numpy as jnp
from jax import lax
from jax.experimental import pallas as pl
from jax.experimental.pallas import tpu as pltpu

F32 = jnp.float32
BF16 = jnp.bfloat16

EPS = 1e-6
N_MOD = 9
FFN_RES_WEIGHT = 0.5
GRID_W = 64
ATT_HEADS = 8
ATT_QK = 64
ATT_V = 2 * ATT_QK
ROPE_THETA = 10000.0
ROPE_PAIRS = ATT_QK // 4
LAMBDA_INIT_BASE = 0.8
LAMBDA_INIT_AMP = 0.6
LAMBDA_INIT_RATE = 0.3
REC_HEADS = 8
REC_DK = 128
REC_DV = 128
REC_CHUNK = 64
CONV_K = 3

LANES = 128
MIB = 1024 * 1024
MOD_ROWS = 8
LOG2E = 1.4426950408889634


def _params(sem, vmem_mib):
    return pltpu.CompilerParams(dimension_semantics=sem, vmem_limit_bytes=vmem_mib * MIB)


def _rms(y, g):
    return y * lax.rsqrt(jnp.mean(y * y, axis=-1, keepdims=True) + EPS) * g


def _dot(a, b):
    return jnp.dot(a, b, preferred_element_type=F32)


def _dot_nt(a, b):
    return lax.dot_general(a, b, (((1,), (1,)), ((), ())), preferred_element_type=F32)


def _dot_tn(a, b):
    return lax.dot_general(a, b, (((0,), (0,)), ((), ())), preferred_element_type=F32)


def _mod_kernel(c_ref, w_ref, b_ref, o_ref):
    cv = c_ref[...]
    s = (cv * jax.nn.sigmoid(cv)).astype(BF16)
    o_ref[...] = _dot(s, w_ref[...].astype(BF16)) + b_ref[...]


def _modulation(cvec, ada_w, ada_b, layer):
    depth, d, n = ada_w.shape
    tn = 1024
    b3 = ada_b.reshape(depth, 1, n)
    out = pl.pallas_call(
        _mod_kernel,
        grid=(n // tn,),
        in_specs=[
            pl.BlockSpec((MOD_ROWS, d), lambda j: (0, 0)),
            pl.BlockSpec((None, d, tn), lambda j: (layer, 0, j)),
            pl.BlockSpec((None, 1, tn), lambda j: (layer, 0, j)),
        ],
        out_specs=pl.BlockSpec((MOD_ROWS, tn), lambda j: (0, j)),
        out_shape=jax.ShapeDtypeStruct((MOD_ROWS, n), F32),
        compiler_params=_params(("arbitrary",), 40),
        name="modulation",
    )(cvec, ada_w, b3)
    return out.reshape(MOD_ROWS, N_MOD, d)


def _ffn_kernel(x_ref, mod_ref, g_ref, wg_ref, wu_ref, wd_ref, o_ref, h_ref, *, k0, gi, nj, tf, tail, rc):
    j = pl.program_id(1)
    row_chunks = [slice(r, r + rc) for r in range(0, x_ref.shape[0], rc)]

    def weights(width):
        w_gu = jnp.concatenate([wg_ref[:, :width].astype(BF16), wu_ref[:, :width].astype(BF16)], axis=1)
        return w_gu, wd_ref[:width, :].astype(BF16)

    def partial_down(h, w):
        gu = _dot(h, w[0])
        width = gu.shape[-1] // 2
        g, u = gu[:, :width], gu[:, width:]
        a = (g * jax.nn.sigmoid(g) * u).astype(BF16)
        return _dot(a, w[1])

    @pl.when(j == 0)
    def _():
        w = weights(tf)
        for rows in row_chunks:
            h = _rms(x_ref[rows, :], g_ref[gi:gi + 1, :]) * (1.0 + mod_ref[k0 + 1:k0 + 2, :]) + mod_ref[k0:k0 + 1, :]
            h = h.astype(BF16)
            h_ref[rows, :] = h
            o_ref[rows, :] = partial_down(h, w)

    @pl.when((j > 0) & (j < nj - 1))
    def _():
        o_ref[...] += partial_down(h_ref[...], weights(tf))

    @pl.when(j == nj - 1)
    def _():
        w = weights(tail)
        gate = FFN_RES_WEIGHT * mod_ref[k0 + 2:k0 + 3, :]
        for rows in row_chunks:
            y = o_ref[rows, :] + partial_down(h_ref[rows, :], w)
            o_ref[rows, :] = x_ref[rows, :] + gate * _rms(y, g_ref[gi + 1:gi + 2, :])


def _ffn_half(x, mod3, row_fn, g_l, wg, wu, wd, layer, half, k0, gi, tm, tf=256):
    t, d = x.shape
    f = wg.shape[-1]
    nj = pl.cdiv(f, tf)
    tail = f - (nj - 1) * tf
    kern = functools.partial(_ffn_kernel, k0=k0, gi=gi, nj=nj, tf=tf, tail=tail, rc=min(tm, 256))
    return pl.pallas_call(
        kern,
        grid=(t // tm, nj),
        in_specs=[
            pl.BlockSpec((tm, d), lambda i, j: (i, 0)),
            pl.BlockSpec((None, N_MOD, d), lambda i, j: (row_fn(i), 0, 0)),
            pl.BlockSpec(g_l.shape, lambda i, j: (0, 0)),
            pl.BlockSpec((None, None, d, tf), lambda i, j: (layer, half, 0, j)),
            pl.BlockSpec((None, None, d, tf), lambda i, j: (layer, half, 0, j)),
            pl.BlockSpec((None, None, tf, d), lambda i, j: (layer, half, j, 0)),
        ],
        out_specs=pl.BlockSpec((tm, d), lambda i, j: (i, 0)),
        out_shape=jax.ShapeDtypeStruct((t, d), F32),
        scratch_shapes=[pltpu.VMEM((tm, d), BF16)],
        compiler_params=_params(("parallel", "arbitrary"), 56),
        name="ffn_half",
    )(x, mod3, g_l, wg, wu, wd)


def _proj_kernel(*refs, k0, gi, n_lo, rope_blocks, rc):
    if rope_blocks:
        x_ref, mod_ref, g_ref, w_ref, cos_ref, sin_ref, lo_ref, hi_ref, h_ref = refs
    else:
        x_ref, mod_ref, g_ref, w_ref, lo_ref, hi_ref, h_ref = refs
    j = pl.program_id(1)
    tn = w_ref.shape[-1]
    assert n_lo >= 1
    row_chunks = [slice(r, r + rc) for r in range(0, x_ref.shape[0], rc)]

    def rotate(p, rows):
        lane = lax.broadcasted_iota(jnp.int32, (1, tn), 1)
        first = (lane % (2 * ROPE_PAIRS)) < ROPE_PAIRS
        partner = jnp.where(first, pltpu.roll(p, tn - ROPE_PAIRS, axis=1), pltpu.roll(p, ROPE_PAIRS, axis=1))
        return p * cos_ref[rows, :] + partner * sin_ref[rows, :]

    @pl.when(j == 0)
    def _():
        w = w_ref[...].astype(BF16)
        for rows in row_chunks:
            h = _rms(x_ref[rows, :], g_ref[gi:gi + 1, :]) * (1.0 + mod_ref[k0 + 1:k0 + 2, :]) + mod_ref[k0:k0 + 1, :]
            h = h.astype(BF16)
            h_ref[rows, :] = h
            p = _dot(h, w)
            lo_ref[rows, :] = (rotate(p, rows) if rope_blocks else p).astype(BF16)

    if rope_blocks > 1:
        @pl.when((j > 0) & (j < rope_blocks))
        def _():
            p = _dot(h_ref[...], w_ref[...].astype(BF16))
            lo_ref[...] = rotate(p, slice(None)).astype(BF16)

    @pl.when((j >= max(rope_blocks, 1)) & (j < n_lo))
    def _():
        lo_ref[...] = _dot(h_ref[...], w_ref[...].astype(BF16)).astype(BF16)

    @pl.when(j >= n_lo)
    def _():
        hi_ref[...] = _dot(h_ref[...], w_ref[...].astype(BF16))


def _proj(x, mod3, row_fn, g_l, w, w_idx, k0, gi, tm, tn, lo_cols, rope=None):
    t, d = x.shape
    n = w.shape[-1]
    n_lo = lo_cols // tn
    in_specs = [
        pl.BlockSpec((tm, d), lambda i, j: (i, 0)),
        pl.BlockSpec((None, N_MOD, d), lambda i, j: (row_fn(i), 0, 0)),
        pl.BlockSpec(g_l.shape, lambda i, j: (0, 0)),
        pl.BlockSpec((None, d, tn), lambda i, j: (w_idx, 0, j)),
    ]
    args = [x, mod3, g_l, w]
    rope_blocks = 0
    if rope is not None:
        cos_t, sin_t, q_cols, k_cols = rope
        seq_tiles = cos_t.shape[1] // tm
        q_blocks = q_cols // tn
        rope_blocks = (q_cols + k_cols) // tn
        in_specs += [pl.BlockSpec((None, tm, tn), lambda i, j: (jnp.minimum(j // q_blocks, 1), i % seq_tiles, 0))] * 2
        args += [cos_t, sin_t]
    kern = functools.partial(_proj_kernel, k0=k0, gi=gi, n_lo=n_lo, rope_blocks=rope_blocks, rc=min(tm, 256))
    return pl.pallas_call(
        kern,
        grid=(t // tm, n // tn),
        in_specs=in_specs,
        out_specs=[pl.BlockSpec((tm, tn), lambda i, j: (i, jnp.minimum(j, n_lo - 1))),
                   pl.BlockSpec((tm, tn), lambda i, j: (i, jnp.maximum(j - n_lo, 0)))],
        out_shape=[jax.ShapeDtypeStruct((t, lo_cols), BF16), jax.ShapeDtypeStruct((t, n - lo_cols), F32)],
        scratch_shapes=[pltpu.VMEM((tm, d), BF16)],
        compiler_params=_params(("parallel", "arbitrary"), 56),
        name="norm_proj",
    )(*args)


def _rope_tables(seq, reps, q_scale):
    n = jnp.arange(seq)
    row = (n // GRID_W).astype(F32)
    col = (n % GRID_W).astype(F32)
    freqs = ROPE_THETA ** (-jnp.arange(ROPE_PAIRS, dtype=F32) / ROPE_PAIRS)

    def half(pos):
        ang = pos[:, None] * freqs
        c, s = jnp.cos(ang), jnp.sin(ang)
        return jnp.concatenate([c, c], axis=-1), jnp.concatenate([-s, s], axis=-1)

    cr, sr = half(row)
    cc, sc = half(col)
    cos_t = jnp.tile(jnp.concatenate([cr, cc], axis=-1), (1, reps))
    sin_t = jnp.tile(jnp.concatenate([sr, sc], axis=-1), (1, reps))
    return jnp.stack([cos_t * q_scale, cos_t]), jnp.stack([sin_t * q_scale, sin_t])


def _attn_kernel(q_ref, kl_ref, vl_ref, kc_ref, vc_ref, lv_ref, gn_ref, o_ref,
                 k_scr, v_scr, s0_scr, s1_scr, w0_scr, w1_scr, *, lam_init, tq, rc):
    n = q_ref.shape[0]
    nc = kc_ref.shape[0]
    nt = n // tq
    assert nt % 2 == 0 and nt >= 4
    k_scr[:nc, :] = kc_ref[...]
    k_scr[nc:, :] = kl_ref[...]
    v_scr[:nc, :] = vc_ref[...]
    v_scr[nc:, :] = vl_ref[...]
    lane = lax.broadcasted_iota(jnp.int32, (1, 2 * ATT_QK), 1)
    lv = lv_ref[...]
    lam = (jnp.exp(jnp.sum(lv[0:1] * lv[1:2], axis=-1, keepdims=True))
           - jnp.exp(jnp.sum(lv[2:3] * lv[3:4], axis=-1, keepdims=True)) + lam_init)
    s_scr = (s0_scr, s1_scr)
    w_scr = (w0_scr, w1_scr)

    def tile_rows(t):
        return pl.ds(pl.multiple_of(t * tq, tq), tq)

    def scores(t, slot):
        q = q_ref[tile_rows(t), :]
        zero = jnp.zeros_like(q)
        k = k_scr[...]
        s_scr[slot][0] = _dot_nt(jnp.where(lane < ATT_QK, q, zero), k)
        s_scr[slot][1] = _dot_nt(jnp.where(lane >= ATT_QK, q, zero), k)

    def weights(slot):
        for r in range(0, tq, rc):
            def numerators(m):
                s = s_scr[slot][m, r:r + rc, :]
                e = jnp.exp2(s - jnp.max(s, axis=-1, keepdims=True))
                return e, 1.0 / jnp.sum(e, axis=-1, keepdims=True)

            e1, inv1 = numerators(0)
            e2, inv2 = numerators(1)
            w_scr[slot][r:r + rc, :] = (e1 * inv1 - e2 * (lam * inv2)).astype(BF16)

    def outputs(t, slot):
        o = _dot(w_scr[slot][...], v_scr[...])
        o_ref[tile_rows(t), :] = (_rms(o, gn_ref[...]) * (1.0 - lam_init)).astype(o_ref.dtype)

    scores(0, 0)
    scores(1, 1)
    weights(0)

    def steady(p, carry):
        t = 2 * p
        scores(t, 0)
        weights(1)
        outputs(t - 2, 0)
        scores(t + 1, 1)
        weights(0)
        outputs(t - 1, 1)
        return carry

    lax.fori_loop(1, nt // 2, steady, 0)
    weights(1)
    outputs(nt - 2, 0)
    outputs(nt - 1, 1)


def _attention(a_lat, a_ctx, lam_vecs, norm_g, lam_init, tq):
    b, n, _ = a_lat.shape
    nc = a_ctx.shape[1]
    kb = ATT_HEADS
    vb = 2 * ATT_HEADS
    kern = functools.partial(_attn_kernel, lam_init=lam_init, tq=tq, rc=min(tq, 256))
    nk = n + nc
    return pl.pallas_call(
        kern,
        grid=(b, ATT_HEADS),
        in_specs=[
            pl.BlockSpec((None, n, LANES), lambda bi, h: (bi, 0, h)),
            pl.BlockSpec((None, n, LANES), lambda bi, h: (bi, 0, kb + h)),
            pl.BlockSpec((None, n, LANES), lambda bi, h: (bi, 0, vb + h)),
            pl.BlockSpec((None, nc, LANES), lambda bi, h: (bi, 0, kb + h)),
            pl.BlockSpec((None, nc, LANES), lambda bi, h: (bi, 0, vb + h)),
            pl.BlockSpec(lam_vecs.shape, lambda bi, h: (0, 0)),
            pl.BlockSpec(norm_g.shape, lambda bi, h: (0, 0)),
        ],
        out_specs=pl.BlockSpec((None, n, LANES), lambda bi, h: (bi, 0, h)),
        out_shape=jax.ShapeDtypeStruct((b, n, ATT_HEADS * ATT_V), BF16),
        scratch_shapes=[
            pltpu.VMEM((nk, LANES), BF16),
            pltpu.VMEM((nk, LANES), BF16),
            pltpu.VMEM((2, tq, nk), F32),
            pltpu.VMEM((2, tq, nk), F32),
            pltpu.VMEM((tq, nk), BF16),
            pltpu.VMEM((tq, nk), BF16),
        ],
        compiler_params=_params(("parallel", "parallel"), 48),
        name="diff_attention",
    )(a_lat, a_lat, a_lat, a_ctx, a_ctx, lam_vecs, norm_g)


def _hgrn_kernel(q_ref, zf_ref, zb_ref, i_ref, g_ref, czf_ref, czb_ref, ci_ref, lb_ref, gn_ref, o_ref,
                 o_scr, qs_scr, u_scr, dl_scr, *, n_lat, n_ctx, slot, rb):
    C = REC_CHUNK
    nct = n_ctx // C
    ncl = n_lat // C
    cpb = rb // C

    def lower_bound(d):
        a = lb_ref[d]
        ex = jnp.exp(a - jnp.max(a, axis=0, keepdims=True))
        return jnp.sum(ex[:slot + 1], axis=0, keepdims=True) / jnp.sum(ex, axis=0, keepdims=True)

    lb = (lower_bound(0), lower_bound(1))
    r_i = lax.broadcasted_iota(jnp.int32, (rb, rb), 0)
    c_i = lax.broadcasted_iota(jnp.int32, (rb, rb), 1)
    same = (r_i // C) == (c_i // C)
    masks = (same & (c_i <= r_i), same & (c_i >= r_i))

    tri = masks[0].astype(BF16)

    def chunk_rows(a, offset):
        return jnp.concatenate(
            [jnp.broadcast_to(a[c * C + offset:c * C + offset + 1, :], (C, LANES)) for c in range(cpb)], axis=0)

    def gate(z, lower):
        f = lower + (1.0 - lower) * jax.nn.sigmoid(z)
        return 1.0 - f, jnp.log(f)

    def prep(zf, zb, v, q, base, blk, rows):
        kk_f, lg_f = gate(zf, lb[0])
        kk_b, lg_b = gate(zb, lb[1])
        lg = jnp.concatenate([lg_f, lg_b], axis=1)
        hi = lg.astype(BF16)
        lo = (lg - hi.astype(F32)).astype(BF16)
        pre = _dot(tri, hi) + _dot(tri, lo)
        cum_f = pre[:, :LANES]
        last_f = chunk_rows(cum_f, C - 1)
        last_b = chunk_rows(pre[:, LANES:], C - 1)
        cum_b = last_b - pre[:, LANES:] + lg_b
        kd = jnp.concatenate([kk_f * jnp.exp(last_f - cum_f), kk_b * jnp.exp(last_b - cum_b)], axis=1).astype(BF16)
        for c in range(cpb):
            idx = base + blk * cpb + c
            u = _dot_tn(v[c * C:(c + 1) * C], kd[c * C:(c + 1) * C])
            u_scr[0, idx] = u[:, :LANES]
            u_scr[1, idx] = u[:, LANES:]
            dl_scr[0, idx] = jnp.exp(last_f[c * C:c * C + 8, :])
            dl_scr[1, idx] = jnp.exp(last_b[c * C:c * C + 8, :])
        if q is None:
            return None

        def intra(d, cum, kk, mid):
            ref = chunk_rows(cum, mid)
            dlt = cum - ref
            qe = q * jnp.exp(dlt)
            ke = kk * jnp.exp(-dlt)
            qs_scr[d, rows, :] = (qe * jnp.exp(ref)).astype(BF16)
            return jnp.where(masks[d], _dot_nt(qe.astype(BF16), ke.astype(BF16)), 0.0)

        att = intra(0, cum_f, kk_f, C // 2 - 1) + intra(1, cum_b, kk_b, C // 2)
        return _dot(att.astype(BF16), v)

    for blk in range(n_ctx // rb):
        rows = slice(blk * rb, (blk + 1) * rb)
        prep(czf_ref[rows, :], czb_ref[rows, :], ci_ref[rows, :].astype(BF16), None, 0, blk, rows)

    def lat_block(blk, carry):
        rows = pl.ds(pl.multiple_of(blk * rb, rb), rb)
        v = i_ref[rows, :].astype(BF16)
        q = q_ref[rows, :] * (REC_DK ** -0.5)
        o_scr[rows, :] = prep(zf_ref[rows, :], zb_ref[rows, :], v, q, nct, blk, rows)
        return carry

    lax.fori_loop(0, n_lat // rb, lat_block, 0, unroll=4)

    def advance(d, idx, st):
        return st * dl_scr[d, idx, 0:1, :] + u_scr[d, idx]

    st_f = jnp.zeros((REC_DV, REC_DK), F32)
    st_b = jnp.zeros((REC_DV, REC_DK), F32)
    for c in range(nct):
        st_f = advance(0, c, st_f)
        st_b = advance(1, nct - 1 - c, st_b)

    def seq_step(s, carry):
        st_f, st_b = carry
        cf = s
        cb = ncl - 1 - s
        rows_f = pl.ds(pl.multiple_of(cf * C, C), C)
        rows_b = pl.ds(pl.multiple_of(cb * C, C), C)
        o_scr[rows_f, :] += _dot_nt(qs_scr[0, rows_f, :], st_f.astype(BF16))
        o_scr[rows_b, :] += _dot_nt(qs_scr[1, rows_b, :], st_b.astype(BF16))
        return advance(0, nct + cf, st_f), advance(1, nct + cb, st_b)

    lax.fori_loop(0, ncl, seq_step, (st_f, st_b), unroll=4)

    g = g_ref[...]
    o_ref[...] = (_rms(o_scr[...], gn_ref[...]) * (g * jax.nn.sigmoid(g))).astype(o_ref.dtype)


def _hgrn(r_lat, r_ctx, rec_lb, norm_g, slot):
    b, n, _ = r_lat.shape
    nc = r_ctx.shape[1]
    rb = math.gcd(256, math.gcd(n, nc))
    nchunks = (n + nc) // REC_CHUNK
    cols = [k * REC_HEADS for k in range(5)]
    lat_spec = lambda c: pl.BlockSpec((None, n, LANES), lambda bi, h: (bi, 0, c + h))
    ctx_spec = lambda c: pl.BlockSpec((None, nc, LANES), lambda bi, h: (bi, 0, c + h))
    kern = functools.partial(_hgrn_kernel, n_lat=n, n_ctx=nc, slot=slot, rb=rb)
    return pl.pallas_call(
        kern,
        grid=(b, REC_HEADS),
        in_specs=[lat_spec(cols[0]), lat_spec(cols[1]), lat_spec(cols[2]), lat_spec(cols[3]), lat_spec(cols[4]),
                  ctx_spec(cols[1]), ctx_spec(cols[2]), ctx_spec(cols[3]),
                  pl.BlockSpec((2, rec_lb.shape[1], LANES), lambda bi, h: (0, 0, h)),
                  pl.BlockSpec(norm_g.shape, lambda bi, h: (0, 0))],
        out_specs=pl.BlockSpec((None, n, LANES), lambda bi, h: (bi, 0, h)),
        out_shape=jax.ShapeDtypeStruct((b, n, REC_HEADS * REC_DV), BF16),
        scratch_shapes=[
            pltpu.VMEM((n, LANES), F32),
            pltpu.VMEM((2, n, LANES), BF16),
            pltpu.VMEM((2, nchunks, REC_DV, REC_DK), F32),
            pltpu.VMEM((2, nchunks, 8, LANES), F32),
        ],
        compiler_params=_params(("parallel", "parallel"), 48),
        name="hgrn2_bidir",
    )(r_lat, r_lat, r_lat, r_lat, r_lat, r_ctx, r_ctx, r_ctx, rec_lb, norm_g)


def _mix_out_kernel(a_ref, r_ref, w_ref, x_ref, mod_ref, g_ref, o_ref, *, k0, gi):
    ka = a_ref.shape[-1]
    y = _dot(a_ref[...], w_ref[:ka, :]) + _dot(r_ref[...], w_ref[ka:, :])
    o_ref[...] = x_ref[...] + mod_ref[k0 + 2:k0 + 3, :] * _rms(y, g_ref[gi + 1:gi + 2, :])


def _mix_out(oa, orec, w_bf, x, mod3, row_fn, g_l, k0, gi, tm):
    t, d = x.shape
    ka, kr = oa.shape[-1], orec.shape[-1]
    kern = functools.partial(_mix_out_kernel, k0=k0, gi=gi)
    return pl.pallas_call(
        kern,
        grid=(t // tm,),
        in_specs=[
            pl.BlockSpec((tm, ka), lambda i: (i, 0)),
            pl.BlockSpec((tm, kr), lambda i: (i, 0)),
            pl.BlockSpec(w_bf.shape, lambda i: (0, 0)),
            pl.BlockSpec((tm, d), lambda i: (i, 0)),
            pl.BlockSpec((None, N_MOD, d), lambda i: (row_fn(i), 0, 0)),
            pl.BlockSpec(g_l.shape, lambda i: (0, 0)),
        ],
        out_specs=pl.BlockSpec((tm, d), lambda i: (i, 0)),
        out_shape=jax.ShapeDtypeStruct((t, d), F32),
        compiler_params=_params(("parallel",), 48),
        name="mixer_out",
    )(oa, orec, w_bf, x, mod3, g_l)


def _conv_in_kernel(x_ref, mod_ref, g_ref, wb_ref, wc_ref, wv_ref, b_ref, cv_ref, h_ref, *, k0, gi, rc):
    j = pl.program_id(1)

    def weights():
        return wb_ref[...].astype(BF16), wc_ref[...].astype(BF16), wv_ref[...].astype(BF16)

    @pl.when(j == 0)
    def _():
        wb, wc, wv = weights()
        for r in range(0, x_ref.shape[0], rc):
            rows = slice(r, r + rc)
            h = _rms(x_ref[rows, :], g_ref[gi:gi + 1, :]) * (1.0 + mod_ref[k0 + 1:k0 + 2, :]) + mod_ref[k0:k0 + 1, :]
            h = h.astype(BF16)
            h_ref[rows, :] = h
            b_ref[rows, :] = _dot(h, wb)
            cv_ref[rows, :] = _dot(h, wc) * _dot(h, wv)

    @pl.when(j > 0)
    def _():
        wb, wc, wv = weights()
        h = h_ref[...]
        b_ref[...] = _dot(h, wb)
        cv_ref[...] = _dot(h, wc) * _dot(h, wv)


def _conv_in(x, mod3, row_fn, g_l, w, w_idx, k0, gi, tm, tn):
    t, d = x.shape
    width = w.shape[-1] // 3
    nb = width // tn
    kern = functools.partial(_conv_in_kernel, k0=k0, gi=gi, rc=min(tm, 256))
    w_spec = lambda off: pl.BlockSpec((None, d, tn), lambda i, j: (w_idx, 0, off + j))
    return pl.pallas_call(
        kern,
        grid=(t // tm, nb),
        in_specs=[
            pl.BlockSpec((tm, d), lambda i, j: (i, 0)),
            pl.BlockSpec((None, N_MOD, d), lambda i, j: (row_fn(i), 0, 0)),
            pl.BlockSpec(g_l.shape, lambda i, j: (0, 0)),
            w_spec(0), w_spec(nb), w_spec(2 * nb),
        ],
        out_specs=[pl.BlockSpec((tm, tn), lambda i, j: (i, j))] * 2,
        out_shape=[jax.ShapeDtypeStruct((t, width), F32)] * 2,
        scratch_shapes=[pltpu.VMEM((tm, d), BF16)],
        compiler_params=_params(("parallel", "arbitrary"), 56),
        name="conv_in",
    )(x, mod3, g_l, w, w, w)


def _conv_out_kernel(b_ref, cv_ref, prev_ref, next_ref, cw_ref, w_ref, x_ref, mod_ref, g_ref, o_ref,
                     *, k0, gi, seq_tiles):
    i = pl.program_id(0)
    tm = cv_ref.shape[0]
    cv = cv_ref[...]
    row = lax.broadcasted_iota(jnp.int32, (tm, 1), 0)
    prev = jnp.where(i % seq_tiles == 0, 0.0, prev_ref[7:8, :])
    nxt = jnp.where(i % seq_tiles == seq_tiles - 1, 0.0, next_ref[0:1, :])
    up = jnp.where(row == 0, prev, pltpu.roll(cv, 1, axis=0))
    dn = jnp.where(row == tm - 1, nxt, pltpu.roll(cv, tm - 1, axis=0))
    u = cw_ref[0:1, :] * up + cw_ref[1:2, :] * cv + cw_ref[2:3, :] * dn
    y = _dot((b_ref[...] * u).astype(BF16), w_ref[...])
    o_ref[...] = x_ref[...] + mod_ref[k0 + 2:k0 + 3, :] * _rms(y, g_ref[gi + 1:gi + 2, :])


def _conv_out(bg, cv, conv_w, w_bf, x, mod3, row_fn, g_l, k0, gi, tm, seq):
    t, d = x.shape
    width = cv.shape[-1]
    halo = 8
    hb = tm // halo
    last = t // halo - 1
    kern = functools.partial(_conv_out_kernel, k0=k0, gi=gi, seq_tiles=seq // tm)
    return pl.pallas_call(
        kern,
        grid=(t // tm,),
        in_specs=[
            pl.BlockSpec((tm, width), lambda i: (i, 0)),
            pl.BlockSpec((tm, width), lambda i: (i, 0)),
            pl.BlockSpec((halo, width), lambda i: (jnp.maximum(i * hb - 1, 0), 0)),
            pl.BlockSpec((halo, width), lambda i: (jnp.minimum((i + 1) * hb, last), 0)),
            pl.BlockSpec(conv_w.shape, lambda i: (0, 0)),
            pl.BlockSpec(w_bf.shape, lambda i: (0, 0)),
            pl.BlockSpec((tm, d), lambda i: (i, 0)),
            pl.BlockSpec((None, N_MOD, d), lambda i: (row_fn(i), 0, 0)),
            pl.BlockSpec(g_l.shape, lambda i: (0, 0)),
        ],
        out_specs=pl.BlockSpec((tm, d), lambda i: (i, 0)),
        out_shape=jax.ShapeDtypeStruct((t, d), F32),
        compiler_params=_params(("parallel",), 48),
        name="conv_out",
    )(bg, cv, cv, cv, conv_w, w_bf, x, mod3, g_l)


def _tile(n, pref):
    t = pref
    while n % t:
        t //= 2
    return t


def kernel(x, c, ctx, c_ctx, ada_w, ada_b, norm_g, ffn_w_gate, ffn_w_up, ffn_w_down, mix_w_in, mix_w_out,
           diff_lambda, diff_norm_g, rec_norm_g, rec_lb, conv_w_in, conv_w, conv_w_out):
    batch, seq, d = x.shape
    ctx_len = ctx.shape[1]
    depth = ada_w.shape[0]
    ctx_row = batch
    assert batch < MOD_ROWS

    cvec = jnp.concatenate([c, c_ctx[None, :], jnp.zeros((MOD_ROWS - batch - 1, d), F32)], axis=0)
    xl = x.reshape(batch * seq, d)
    xc = ctx.reshape(batch * ctx_len, d)

    tm_l = _tile(seq, 1024)
    tm_c = _tile(batch * ctx_len, 1024)
    lat_row = lambda tm: (lambda i: i // (seq // tm))
    ctx_rowf = lambda i: ctx_row

    for l in range(depth):
        even = l % 2 == 0
        ctx_out = any(j % 2 == 0 for j in range(l + 1, depth))
        ctx_in = even or ctx_out
        g_l = norm_g[l]
        mod3 = _modulation(cvec, ada_w, ada_b, l)
        ffn = lambda xx, row_fn, half, k0, gi, tm: _ffn_half(
            xx, mod3, row_fn, g_l, ffn_w_gate, ffn_w_up, ffn_w_down, l, half, k0, gi, tm)

        xl = ffn(xl, lat_row(tm_l), 0, 0, 0, tm_l)
        if ctx_in:
            xc = ffn(xc, ctx_rowf, 0, 0, 0, tm_c)

        tm_o = _tile(seq, 512)
        if even:
            e = l // 2
            lam_init = LAMBDA_INIT_BASE - LAMBDA_INIT_AMP * math.exp(-LAMBDA_INIT_RATE * l)
            q_cols = k_cols = ATT_HEADS * 2 * ATT_QK
            att_cols = q_cols + k_cols + ATT_HEADS * ATT_V
            tn = 512
            rope = _rope_tables(seq, tn // ATT_QK, ATT_QK ** -0.5 * LOG2E) + (q_cols, k_cols)
            a_lat, r_lat = _proj(xl, mod3, lat_row(tm_l), g_l, mix_w_in, e, 3, 2, tm_l, tn, att_cols, rope=rope)
            a_ctx, r_ctx = _proj(xc, mod3, ctx_rowf, g_l, mix_w_in, e, 3, 2, tm_c, tn, att_cols)
            lat3 = lambda a: a.reshape(batch, seq, -1)
            ctx3 = lambda a: a.reshape(batch, ctx_len, -1)
            oa = _attention(lat3(a_lat), ctx3(a_ctx), diff_lambda[e], diff_norm_g[e][None, :], lam_init,
                            min(256, seq // 4))
            orec = _hgrn(lat3(r_lat), ctx3(r_ctx), rec_lb, rec_norm_g[e][None, :], e)
            w_bf = mix_w_out[e].astype(BF16)
            xl_new = _mix_out(oa.reshape(batch * seq, -1), orec.reshape(batch * seq, -1), w_bf, xl, mod3,
                              lat_row(tm_o), g_l, 3, 2, tm_o)
            if ctx_out:
                raise NotImplementedError("context mixer output is not needed for this depth")
        else:
            o = l // 2
            bg, cv = _conv_in(xl, mod3, lat_row(tm_l), g_l, conv_w_in, o, 3, 2, tm_l, 256)
            xl_new = _conv_out(bg, cv, conv_w[o], conv_w_out[o].astype(BF16), xl, mod3, lat_row(tm_o), g_l,
                               3, 2, tm_o, seq)
            if ctx_out:
                raise NotImplementedError("context mixer output is not needed for this depth")
        xl = xl_new
        xl = ffn(xl, lat_row(tm_l), 1, 6, 4, tm_l)
    return xl.reshape(batch, seq, d)
```

```python
import functools
import math

import jax
import jax.numpy as jnp
from jax import lax
from jax.experimental import pallas as pl
from jax.experimental.pallas import tpu as pltpu

F32 = jnp.float32
BF16 = jnp.bfloat16

EPS = 1e-6
N_MOD = 9
FFN_RES_WEIGHT = 0.5
GRID_W = 64
ATT_HEADS = 8
ATT_QK = 64
ATT_V = 2 * ATT_QK
ROPE_THETA = 10000.0
ROPE_PAIRS = ATT_QK // 4
LAMBDA_INIT_BASE = 0.8
LAMBDA_INIT_AMP = 0.6
LAMBDA_INIT_RATE = 0.3
REC_HEADS = 8
REC_DK = 128
REC_DV = 128
REC_CHUNK = 64
CONV_K = 3

LANES = 128
MIB = 1024 * 1024
MOD_ROWS = 8
LOG2E = 1.4426950408889634


def _params(sem, vmem_mib):
    return pltpu.CompilerParams(dimension_semantics=sem, vmem_limit_bytes=vmem_mib * MIB)


def _rms(y, g):
    return y * lax.rsqrt(jnp.mean(y * y, axis=-1, keepdims=True) + EPS) * g


def _dot(a, b):
    return jnp.dot(a, b, preferred_element_type=F32)


def _dot_nt(a, b):
    return lax.dot_general(a, b, (((1,), (1,)), ((), ())), preferred_element_type=F32)


def _dot_tn(a, b):
    return lax.dot_general(a, b, (((0,), (0,)), ((), ())), preferred_element_type=F32)


def _mod_kernel(c_ref, w_ref, b_ref, o_ref):
    cv = c_ref[...]
    s = (cv * jax.nn.sigmoid(cv)).astype(BF16)
    o_ref[...] = _dot(s, w_ref[...].astype(BF16)) + b_ref[...]


def _modulation(cvec, ada_w, ada_b, layer):
    depth, d, n = ada_w.shape
    tn = 1024
    b3 = ada_b.reshape(depth, 1, n)
    out = pl.pallas_call(
        _mod_kernel,
        grid=(n // tn,),
        in_specs=[
            pl.BlockSpec((MOD_ROWS, d), lambda j: (0, 0)),
            pl.BlockSpec((None, d, tn), lambda j: (layer, 0, j)),
            pl.BlockSpec((None, 1, tn), lambda j: (layer, 0, j)),
        ],
        out_specs=pl.BlockSpec((MOD_ROWS, tn), lambda j: (0, j)),
        out_shape=jax.ShapeDtypeStruct((MOD_ROWS, n), F32),
        compiler_params=_params(("arbitrary",), 40),
        name="modulation",
    )(cvec, ada_w, b3)
    return out.reshape(MOD_ROWS, N_MOD, d)


def _ffn_kernel(x_ref, mod_ref, g_ref, wg_ref, wu_ref, wd_ref, o_ref, h_ref, *, k0, gi, nj, tf, tail, rc):
    j = pl.program_id(1)
    row_chunks = [slice(r, r + rc) for r in range(0, x_ref.shape[0], rc)]

    def weights(width):
        w_gu = jnp.concatenate([wg_ref[:, :width].astype(BF16), wu_ref[:, :width].astype(BF16)], axis=1)
        return w_gu, wd_ref[:width, :].astype(BF16)

    def partial_down(h, w):
        gu = _dot(h, w[0])
        width = gu.shape[-1] // 2
        g, u = gu[:, :width], gu[:, width:]
        a = (g * jax.nn.sigmoid(g) * u).astype(BF16)
        return _dot(a, w[1])

    @pl.when(j == 0)
    def _():
        w = weights(tf)
        for rows in row_chunks:
            h = _rms(x_ref[rows, :], g_ref[gi:gi + 1, :]) * (1.0 + mod_ref[k0 + 1:k0 + 2, :]) + mod_ref[k0:k0 + 1, :]
            h = h.astype(BF16)
            h_ref[rows, :] = h
            o_ref[rows, :] = partial_down(h, w)

    @pl.when((j > 0) & (j < nj - 1))
    def _():
        o_ref[...] += partial_down(h_ref[...], weights(tf))

    @pl.when(j == nj - 1)
    def _():
        w = weights(tail)
        gate = FFN_RES_WEIGHT * mod_ref[k0 + 2:k0 + 3, :]
        for rows in row_chunks:
            y = o_ref[rows, :] + partial_down(h_ref[rows, :], w)
            o_ref[rows, :] = x_ref[rows, :] + gate * _rms(y, g_ref[gi + 1:gi + 2, :])


def _ffn_half(x, mod3, row_fn, g_l, wg, wu, wd, layer, half, k0, gi, tm, tf=256):
    t, d = x.shape
    f = wg.shape[-1]
    nj = pl.cdiv(f, tf)
    tail = f - (nj - 1) * tf
    kern = functools.partial(_ffn_kernel, k0=k0, gi=gi, nj=nj, tf=tf, tail=tail, rc=min(tm, 256))
    return pl.pallas_call(
        kern,
        grid=(t // tm, nj),
        in_specs=[
            pl.BlockSpec((tm, d), lambda i, j: (i, 0)),
            pl.BlockSpec((None, N_MOD, d), lambda i, j: (row_fn(i), 0, 0)),
            pl.BlockSpec(g_l.shape, lambda i, j: (0, 0)),
            pl.BlockSpec((None, None, d, tf), lambda i, j: (layer, half, 0, j)),
            pl.BlockSpec((None, None, d, tf), lambda i, j: (layer, half, 0, j)),
            pl.BlockSpec((None, None, tf, d), lambda i, j: (layer, half, j, 0)),
        ],
        out_specs=pl.BlockSpec((tm, d), lambda i, j: (i, 0)),
        out_shape=jax.ShapeDtypeStruct((t, d), F32),
        scratch_shapes=[pltpu.VMEM((tm, d), BF16)],
        compiler_params=_params(("parallel", "arbitrary"), 56),
        name="ffn_half",
    )(x, mod3, g_l, wg, wu, wd)


def _proj_kernel(*refs, k0, gi, n_lo, rope_blocks, rc):
    if rope_blocks:
        x_ref, mod_ref, g_ref, w_ref, cos_ref, sin_ref, lo_ref, hi_ref, h_ref = refs
    else:
        x_ref, mod_ref, g_ref, w_ref, lo_ref, hi_ref, h_ref = refs
    j = pl.program_id(1)
    tn = w_ref.shape[-1]
    assert n_lo >= 1
    row_chunks = [slice(r, r + rc) for r in range(0, x_ref.shape[0], rc)]

    def rotate(p, rows):
        lane = lax.broadcasted_iota(jnp.int32, (1, tn), 1)
        first = (lane % (2 * ROPE_PAIRS)) < ROPE_PAIRS
        partner = jnp.where(first, pltpu.roll(p, tn - ROPE_PAIRS, axis=1), pltpu.roll(p, ROPE_PAIRS, axis=1))
        reps = (1, tn // LANES)
        return p * jnp.tile(cos_ref[rows, :], reps) + partner * jnp.tile(sin_ref[rows, :], reps)

    @pl.when(j == 0)
    def _():
        w = w_ref[...].astype(BF16)
        for rows in row_chunks:
            h = _rms(x_ref[rows, :], g_ref[gi:gi + 1, :]) * (1.0 + mod_ref[k0 + 1:k0 + 2, :]) + mod_ref[k0:k0 + 1, :]
            h = h.astype(BF16)
            h_ref[rows, :] = h
            p = _dot(h, w)
            lo_ref[rows, :] = (rotate(p, rows) if rope_blocks else p).astype(BF16)

    if rope_blocks > 1:
        @pl.when((j > 0) & (j < rope_blocks))
        def _():
            w = w_ref[...].astype(BF16)
            for rows in row_chunks:
                lo_ref[rows, :] = rotate(_dot(h_ref[rows, :], w), rows).astype(BF16)

    @pl.when((j >= max(rope_blocks, 1)) & (j < n_lo))
    def _():
        lo_ref[...] = _dot(h_ref[...], w_ref[...].astype(BF16)).astype(BF16)

    @pl.when(j >= n_lo)
    def _():
        hi_ref[...] = _dot(h_ref[...], w_ref[...].astype(BF16))


def _proj(x, mod3, row_fn, g_l, w, w_idx, k0, gi, tm, tn, lo_cols, rope=None):
    t, d = x.shape
    n = w.shape[-1]
    n_lo = lo_cols // tn
    in_specs = [
        pl.BlockSpec((tm, d), lambda i, j: (i, 0), pipeline_mode=pl.Buffered(1 if tm > 1024 else 2)),
        pl.BlockSpec((None, N_MOD, d), lambda i, j: (row_fn(i), 0, 0)),
        pl.BlockSpec(g_l.shape, lambda i, j: (0, 0)),
        pl.BlockSpec((None, d, tn), lambda i, j: (w_idx, 0, j)),
    ]
    args = [x, mod3, g_l, w]
    rope_blocks = 0
    if rope is not None:
        cos_t, sin_t, q_cols, k_cols = rope
        seq_tiles = cos_t.shape[1] // tm
        q_blocks = q_cols // tn
        rope_blocks = (q_cols + k_cols) // tn
        in_specs += [pl.BlockSpec((None, tm, LANES),
                                  lambda i, j: (jnp.minimum(j // q_blocks, 1), i % seq_tiles, 0))] * 2
        args += [cos_t, sin_t]
    kern = functools.partial(_proj_kernel, k0=k0, gi=gi, n_lo=n_lo, rope_blocks=rope_blocks, rc=min(tm, 256))
    return pl.pallas_call(
        kern,
        grid=(t // tm, n // tn),
        in_specs=in_specs,
        out_specs=[pl.BlockSpec((tm, tn), lambda i, j: (i, jnp.minimum(j, n_lo - 1))),
                   pl.BlockSpec((tm, tn), lambda i, j: (i, jnp.maximum(j - n_lo, 0)))],
        out_shape=[jax.ShapeDtypeStruct((t, lo_cols), BF16), jax.ShapeDtypeStruct((t, n - lo_cols), F32)],
        scratch_shapes=[pltpu.VMEM((tm, d), BF16)],
        compiler_params=_params(("parallel", "arbitrary"), 60),
        name="norm_proj",
    )(*args)


def _rope_tables(seq, reps, q_scale):
    n = jnp.arange(seq)
    row = (n // GRID_W).astype(F32)
    col = (n % GRID_W).astype(F32)
    freqs = ROPE_THETA ** (-jnp.arange(ROPE_PAIRS, dtype=F32) / ROPE_PAIRS)

    def half(pos):
        ang = pos[:, None] * freqs
        c, s = jnp.cos(ang), jnp.sin(ang)
        return jnp.concatenate([c, c], axis=-1), jnp.concatenate([-s, s], axis=-1)

    cr, sr = half(row)
    cc, sc = half(col)
    cos_t = jnp.tile(jnp.concatenate([cr, cc], axis=-1), (1, reps))
    sin_t = jnp.tile(jnp.concatenate([sr, sc], axis=-1), (1, reps))
    return jnp.stack([cos_t * q_scale, cos_t]), jnp.stack([sin_t * q_scale, sin_t])


def _attn_kernel(q_ref, kl_ref, vl_ref, kc_ref, vc_ref, lv_ref, gn_ref, o_ref,
                 k_scr, v_scr, s0_scr, s1_scr, w0_scr, w1_scr, *, lam_init, tq):
    n = q_ref.shape[0]
    nc = kc_ref.shape[0]
    nt = n // tq
    assert nt % 2 == 0 and nt >= 4
    k_scr[:nc, :] = kc_ref[...]
    k_scr[nc:, :] = kl_ref[...]
    v_scr[:nc, :] = vc_ref[...]
    v_scr[nc:, :] = vl_ref[...]
    lane = lax.broadcasted_iota(jnp.int32, (1, 2 * ATT_QK), 1)
    lv = lv_ref[...]
    lam = (jnp.exp(jnp.sum(lv[0:1] * lv[1:2], axis=-1, keepdims=True))
           - jnp.exp(jnp.sum(lv[2:3] * lv[3:4], axis=-1, keepdims=True)) + lam_init)
    s_scr = (s0_scr, s1_scr)
    w_scr = (w0_scr, w1_scr)

    def tile_rows(t):
        return pl.ds(pl.multiple_of(t * tq, tq), tq)

    def scores(t, slot):
        q = q_ref[tile_rows(t), :]
        zero = jnp.zeros_like(q)
        k = k_scr[...]
        s_scr[slot][0] = _dot_nt(jnp.where(lane < ATT_QK, q, zero), k)
        s_scr[slot][1] = _dot_nt(jnp.where(lane >= ATT_QK, q, zero), k)

    def weights(slot):
        def numerators(m):
            s = s_scr[slot][m]
            e = jnp.exp2(s - jnp.max(s, axis=-1, keepdims=True))
            return e, 1.0 / jnp.sum(e, axis=-1, keepdims=True)

        e1, inv1 = numerators(0)
        e2, inv2 = numerators(1)
        w_scr[slot][...] = (e1 * inv1 - e2 * (lam * inv2)).astype(BF16)

    def outputs(t, slot):
        o = _dot(w_scr[slot][...], v_scr[...])
        o_ref[tile_rows(t), :] = (_rms(o, gn_ref[...]) * (1.0 - lam_init)).astype(o_ref.dtype)

    scores(0, 0)
    scores(1, 1)
    weights(0)

    def steady(p, carry):
        t = 2 * p
        scores(t, 0)
        weights(1)
        outputs(t - 2, 0)
        scores(t + 1, 1)
        weights(0)
        outputs(t - 1, 1)
        return carry

    lax.fori_loop(1, nt // 2, steady, 0)
    weights(1)
    outputs(nt - 2, 0)
    outputs(nt - 1, 1)


def _attention(a_lat, a_ctx, lam_vecs, norm_g, lam_init, tq):
    b, n, _ = a_lat.shape
    nc = a_ctx.shape[1]
    kb = ATT_HEADS
    vb = 2 * ATT_HEADS
    kern = functools.partial(_attn_kernel, lam_init=lam_init, tq=tq)
    nk = n + nc
    return pl.pallas_call(
        kern,
        grid=(b, ATT_HEADS),
        in_specs=[
            pl.BlockSpec((None, n, LANES), lambda bi, h: (bi, 0, h)),
            pl.BlockSpec((None, n, LANES), lambda bi, h: (bi, 0, kb + h)),
            pl.BlockSpec((None, n, LANES), lambda bi, h: (bi, 0, vb + h)),
            pl.BlockSpec((None, nc, LANES), lambda bi, h: (bi, 0, kb + h)),
            pl.BlockSpec((None, nc, LANES), lambda bi, h: (bi, 0, vb + h)),
            pl.BlockSpec(lam_vecs.shape, lambda bi, h: (0, 0)),
            pl.BlockSpec(norm_g.shape, lambda bi, h: (0, 0)),
        ],
        out_specs=pl.BlockSpec((None, n, LANES), lambda bi, h: (bi, 0, h)),
        out_shape=jax.ShapeDtypeStruct((b, n, ATT_HEADS * ATT_V), BF16),
        scratch_shapes=[
            pltpu.VMEM((nk, LANES), BF16),
            pltpu.VMEM((nk, LANES), BF16),
            pltpu.VMEM((2, tq, nk), F32),
            pltpu.VMEM((2, tq, nk), F32),
            pltpu.VMEM((tq, nk), BF16),
            pltpu.VMEM((tq, nk), BF16),
        ],
        compiler_params=_params(("parallel", "parallel"), 48),
        name="diff_attention",
    )(a_lat, a_lat, a_lat, a_ctx, a_ctx, lam_vecs, norm_g)


def _hgrn_kernel(q_ref, zf_ref, zb_ref, i_ref, g_ref, czf_ref, czb_ref, ci_ref, lb_ref, gn_ref, o_ref,
                 o_scr, qs_scr, u_scr, dl_scr, *, n_lat, n_ctx, slot, rb):
    C = REC_CHUNK
    nct = n_ctx // C
    ncl = n_lat // C
    cpb = rb // C

    def lower_bound(d):
        a = lb_ref[d]
        ex = jnp.exp(a - jnp.max(a, axis=0, keepdims=True))
        return jnp.sum(ex[:slot + 1], axis=0, keepdims=True) / jnp.sum(ex, axis=0, keepdims=True)

    lb = (lower_bound(0), lower_bound(1))
    r_i = lax.broadcasted_iota(jnp.int32, (rb, rb), 0)
    c_i = lax.broadcasted_iota(jnp.int32, (rb, rb), 1)
    same = (r_i // C) == (c_i // C)
    masks = (same & (c_i <= r_i), same & (c_i >= r_i))

    tri = masks[0].astype(BF16)

    def chunk_rows(a, offset):
        return jnp.concatenate(
            [jnp.broadcast_to(a[c * C + offset:c * C + offset + 1, :], (C, LANES)) for c in range(cpb)], axis=0)

    def gate(z, lower):
        f = lower + (1.0 - lower) * jax.nn.sigmoid(z)
        return 1.0 - f, jnp.log(f)

    def prep(zf, zb, v, q, base, blk, rows):
        kk_f, lg_f = gate(zf, lb[0])
        kk_b, lg_b = gate(zb, lb[1])
        lg = jnp.concatenate([lg_f, lg_b], axis=1)
        hi = lg.astype(BF16)
        lo = (lg - hi.astype(F32)).astype(BF16)
        pre = _dot(tri, hi) + _dot(tri, lo)
        cum_f = pre[:, :LANES]
        last_f = chunk_rows(cum_f, C - 1)
        last_b = chunk_rows(pre[:, LANES:], C - 1)
        cum_b = last_b - pre[:, LANES:] + lg_b
        kd = jnp.concatenate([kk_f * jnp.exp(last_f - cum_f), kk_b * jnp.exp(last_b - cum_b)], axis=1).astype(BF16)
        for c in range(cpb):
            idx = base + blk * cpb + c
            u = _dot_tn(v[c * C:(c + 1) * C], kd[c * C:(c + 1) * C])
            u_scr[0, idx] = u[:, :LANES]
            u_scr[1, idx] = u[:, LANES:]
            dl_scr[0, idx] = jnp.exp(last_f[c * C:c * C + 8, :])
            dl_scr[1, idx] = jnp.exp(last_b[c * C:c * C + 8, :])
        if q is None:
            return None

        def intra(d, cum, kk, mid):
            ref = chunk_rows(cum, mid)
            dlt = cum - ref
            qe = q * jnp.exp(dlt)
            ke = kk * jnp.exp(-dlt)
            qs_scr[d, rows, :] = (qe * jnp.exp(ref)).astype(BF16)
            return jnp.where(masks[d], _dot_nt(qe.astype(BF16), ke.astype(BF16)), 0.0)

        att = intra(0, cum_f, kk_f, C // 2 - 1) + intra(1, cum_b, kk_b, C // 2)
        return _dot(att.astype(BF16), v)

    for blk in range(n_ctx // rb):
        rows = slice(blk * rb, (blk + 1) * rb)
        prep(czf_ref[rows, :], czb_ref[rows, :], ci_ref[rows, :].astype(BF16), None, 0, blk, rows)

    def lat_block(blk, carry):
        rows = pl.ds(pl.multiple_of(blk * rb, rb), rb)
        v = i_ref[rows, :].astype(BF16)
        q = q_ref[rows, :] * (REC_DK ** -0.5)
        o_scr[rows, :] = prep(zf_ref[rows, :], zb_ref[rows, :], v, q, nct, blk, rows)
        return carry

    lax.fori_loop(0, n_lat // rb, lat_block, 0, unroll=4)

    def advance(d, idx, st):
        return st * dl_scr[d, idx, 0:1, :] + u_scr[d, idx]

    st_f = jnp.zeros((REC_DV, REC_DK), F32)
    st_b = jnp.zeros((REC_DV, REC_DK), F32)
    for c in range(nct):
        st_f = advance(0, c, st_f)
        st_b = advance(1, nct - 1 - c, st_b)

    def seq_step(s, carry):
        st_f, st_b = carry
        cf = s
        cb = ncl - 1 - s
        rows_f = pl.ds(pl.multiple_of(cf * C, C), C)
        rows_b = pl.ds(pl.multiple_of(cb * C, C), C)
        o_scr[rows_f, :] += _dot_nt(qs_scr[0, rows_f, :], st_f.astype(BF16))
        o_scr[rows_b, :] += _dot_nt(qs_scr[1, rows_b, :], st_b.astype(BF16))
        return advance(0, nct + cf, st_f), advance(1, nct + cb, st_b)

    lax.fori_loop(0, ncl, seq_step, (st_f, st_b), unroll=4)

    g = g_ref[...]
    o_ref[...] = (_rms(o_scr[...], gn_ref[...]) * (g * jax.nn.sigmoid(g))).astype(o_ref.dtype)


def _hgrn(r_lat, r_ctx, rec_lb, norm_g, slot):
    b, n, _ = r_lat.shape
    nc = r_ctx.shape[1]
    rb = math.gcd(256, math.gcd(n, nc))
    nchunks = (n + nc) // REC_CHUNK
    cols = [k * REC_HEADS for k in range(5)]
    lat_spec = lambda c: pl.BlockSpec((None, n, LANES), lambda bi, h: (bi, 0, c + h))
    ctx_spec = lambda c: pl.BlockSpec((None, nc, LANES), lambda bi, h: (bi, 0, c + h))
    kern = functools.partial(_hgrn_kernel, n_lat=n, n_ctx=nc, slot=slot, rb=rb)
    return pl.pallas_call(
        kern,
        grid=(b, REC_HEADS),
        in_specs=[lat_spec(cols[0]), lat_spec(cols[1]), lat_spec(cols[2]), lat_spec(cols[3]), lat_spec(cols[4]),
                  ctx_spec(cols[1]), ctx_spec(cols[2]), ctx_spec(cols[3]),
                  pl.BlockSpec((2, rec_lb.shape[1], LANES), lambda bi, h: (0, 0, h)),
                  pl.BlockSpec(norm_g.shape, lambda bi, h: (0, 0))],
        out_specs=pl.BlockSpec((None, n, LANES), lambda bi, h: (bi, 0, h)),
        out_shape=jax.ShapeDtypeStruct((b, n, REC_HEADS * REC_DV), BF16),
        scratch_shapes=[
            pltpu.VMEM((n, LANES), F32),
            pltpu.VMEM((2, n, LANES), BF16),
            pltpu.VMEM((2, nchunks, REC_DV, REC_DK), F32),
            pltpu.VMEM((2, nchunks, 8, LANES), F32),
        ],
        compiler_params=_params(("parallel", "parallel"), 48),
        name="hgrn2_bidir",
    )(r_lat, r_lat, r_lat, r_lat, r_lat, r_ctx, r_ctx, r_ctx, rec_lb, norm_g)


def _mix_out_kernel(a_ref, r_ref, w_ref, x_ref, mod_ref, g_ref, o_ref, *, k0, gi):
    ka = a_ref.shape[-1]
    y = _dot(a_ref[...], w_ref[:ka, :]) + _dot(r_ref[...], w_ref[ka:, :])
    o_ref[...] = x_ref[...] + mod_ref[k0 + 2:k0 + 3, :] * _rms(y, g_ref[gi + 1:gi + 2, :])


def _mix_out(oa, orec, w_bf, x, mod3, row_fn, g_l, k0, gi, tm):
    t, d = x.shape
    ka, kr = oa.shape[-1], orec.shape[-1]
    kern = functools.partial(_mix_out_kernel, k0=k0, gi=gi)
    return pl.pallas_call(
        kern,
        grid=(t // tm,),
        in_specs=[
            pl.BlockSpec((tm, ka), lambda i: (i, 0)),
            pl.BlockSpec((tm, kr), lambda i: (i, 0)),
            pl.BlockSpec(w_bf.shape, lambda i: (0, 0)),
            pl.BlockSpec((tm, d), lambda i: (i, 0)),
            pl.BlockSpec((None, N_MOD, d), lambda i: (row_fn(i), 0, 0)),
            pl.BlockSpec(g_l.shape, lambda i: (0, 0)),
        ],
        out_specs=pl.BlockSpec((tm, d), lambda i: (i, 0)),
        out_shape=jax.ShapeDtypeStruct((t, d), F32),
        compiler_params=_params(("parallel",), 48),
        name="mixer_out",
    )(oa, orec, w_bf, x, mod3, g_l)


def _conv_in_kernel(x_ref, mod_ref, g_ref, wb_ref, wc_ref, wv_ref, b_ref, cv_ref, h_ref, *, k0, gi, rc):
    j = pl.program_id(1)

    def weights():
        return wb_ref[...].astype(BF16), wc_ref[...].astype(BF16), wv_ref[...].astype(BF16)

    @pl.when(j == 0)
    def _():
        wb, wc, wv = weights()
        for r in range(0, x_ref.shape[0], rc):
            rows = slice(r, r + rc)
            h = _rms(x_ref[rows, :], g_ref[gi:gi + 1, :]) * (1.0 + mod_ref[k0 + 1:k0 + 2, :]) + mod_ref[k0:k0 + 1, :]
            h = h.astype(BF16)
            h_ref[rows, :] = h
            b_ref[rows, :] = _dot(h, wb)
            cv_ref[rows, :] = _dot(h, wc) * _dot(h, wv)

    @pl.when(j > 0)
    def _():
        wb, wc, wv = weights()
        h = h_ref[...]
        b_ref[...] = _dot(h, wb)
        cv_ref[...] = _dot(h, wc) * _dot(h, wv)


def _conv_in(x, mod3, row_fn, g_l, w, w_idx, k0, gi, tm, tn):
    t, d = x.shape
    width = w.shape[-1] // 3
    nb = width // tn
    kern = functools.partial(_conv_in_kernel, k0=k0, gi=gi, rc=min(tm, 256))
    w_spec = lambda off: pl.BlockSpec((None, d, tn), lambda i, j: (w_idx, 0, off + j))
    return pl.pallas_call(
        kern,
        grid=(t // tm, nb),
        in_specs=[
            pl.BlockSpec((tm, d), lambda i, j: (i, 0), pipeline_mode=pl.Buffered(1 if tm > 1024 else 2)),
            pl.BlockSpec((None, N_MOD, d), lambda i, j: (row_fn(i), 0, 0)),
            pl.BlockSpec(g_l.shape, lambda i, j: (0, 0)),
            w_spec(0), w_spec(nb), w_spec(2 * nb),
        ],
        out_specs=[pl.BlockSpec((tm, tn), lambda i, j: (i, j))] * 2,
        out_shape=[jax.ShapeDtypeStruct((t, width), F32)] * 2,
        scratch_shapes=[pltpu.VMEM((tm, d), BF16)],
        compiler_params=_params(("parallel", "arbitrary"), 60),
        name="conv_in",
    )(x, mod3, g_l, w, w, w)


def _conv_out_kernel(b_ref, cv_ref, prev_ref, next_ref, cw_ref, w_ref, x_ref, mod_ref, g_ref, o_ref,
                     *, k0, gi, seq_tiles):
    i = pl.program_id(0)
    tm = cv_ref.shape[0]
    cv = cv_ref[...]
    row = lax.broadcasted_iota(jnp.int32, (tm, 1), 0)
    prev = jnp.where(i % seq_tiles == 0, 0.0, prev_ref[7:8, :])
    nxt = jnp.where(i % seq_tiles == seq_tiles - 1, 0.0, next_ref[0:1, :])
    up = jnp.where(row == 0, prev, pltpu.roll(cv, 1, axis=0))
    dn = jnp.where(row == tm - 1, nxt, pltpu.roll(cv, tm - 1, axis=0))
    u = cw_ref[0:1, :] * up + cw_ref[1:2, :] * cv + cw_ref[2:3, :] * dn
    y = _dot((b_ref[...] * u).astype(BF16), w_ref[...])
    o_ref[...] = x_ref[...] + mod_ref[k0 + 2:k0 + 3, :] * _rms(y, g_ref[gi + 1:gi + 2, :])


def _conv_out(bg, cv, conv_w, w_bf, x, mod3, row_fn, g_l, k0, gi, tm, seq):
    t, d = x.shape
    width = cv.shape[-1]
    halo = 8
    hb = tm // halo
    last = t // halo - 1
    kern = functools.partial(_conv_out_kernel, k0=k0, gi=gi, seq_tiles=seq // tm)
    return pl.pallas_call(
        kern,
        grid=(t // tm,),
        in_specs=[
            pl.BlockSpec((tm, width), lambda i: (i, 0)),
            pl.BlockSpec((tm, width), lambda i: (i, 0)),
            pl.BlockSpec((halo, width), lambda i: (jnp.maximum(i * hb - 1, 0), 0)),
            pl.BlockSpec((halo, width), lambda i: (jnp.minimum((i + 1) * hb, last), 0)),
            pl.BlockSpec(conv_w.shape, lambda i: (0, 0)),
            pl.BlockSpec(w_bf.shape, lambda i: (0, 0)),
            pl.BlockSpec((tm, d), lambda i: (i, 0)),
            pl.BlockSpec((None, N_MOD, d), lambda i: (row_fn(i), 0, 0)),
            pl.BlockSpec(g_l.shape, lambda i: (0, 0)),
        ],
        out_specs=pl.BlockSpec((tm, d), lambda i: (i, 0)),
        out_shape=jax.ShapeDtypeStruct((t, d), F32),
        compiler_params=_params(("parallel",), 48),
        name="conv_out",
    )(bg, cv, cv, cv, conv_w, w_bf, x, mod3, g_l)


def _tile(n, pref):
    t = pref
    while n % t:
        t //= 2
    return t


def kernel(x, c, ctx, c_ctx, ada_w, ada_b, norm_g, ffn_w_gate, ffn_w_up, ffn_w_down, mix_w_in, mix_w_out,
           diff_lambda, diff_norm_g, rec_norm_g, rec_lb, conv_w_in, conv_w, conv_w_out):
    batch, seq, d = x.shape
    ctx_len = ctx.shape[1]
    depth = ada_w.shape[0]
    ctx_row = batch
    assert batch < MOD_ROWS

    cvec = jnp.concatenate([c, c_ctx[None, :], jnp.zeros((MOD_ROWS - batch - 1, d), F32)], axis=0)
    xl = x.reshape(batch * seq, d)
    xc = ctx.reshape(batch * ctx_len, d)

    tm_l = _tile(seq, 1024)
    tm_c = _tile(batch * ctx_len, 1024)
    lat_row = lambda tm: (lambda i: i // (seq // tm))
    ctx_rowf = lambda i: ctx_row

    for l in range(depth):
        even = l % 2 == 0
        ctx_out = any(j % 2 == 0 for j in range(l + 1, depth))
        ctx_in = even or ctx_out
        g_l = norm_g[l]
        mod3 = _modulation(cvec, ada_w, ada_b, l)
        ffn = lambda xx, row_fn, half, k0, gi, tm: _ffn_half(
            xx, mod3, row_fn, g_l, ffn_w_gate, ffn_w_up, ffn_w_down, l, half, k0, gi, tm)

        xl = ffn(xl, lat_row(tm_l), 0, 0, 0, tm_l)
        if ctx_in:
            xc = ffn(xc, ctx_rowf, 0, 0, 0, tm_c)

        tm_o = _tile(seq, 512)
        tm_p = _tile(seq, 2048)
        if even:
            e = l // 2
            lam_init = LAMBDA_INIT_BASE - LAMBDA_INIT_AMP * math.exp(-LAMBDA_INIT_RATE * l)
            q_cols = k_cols = ATT_HEADS * 2 * ATT_QK
            att_cols = q_cols + k_cols + ATT_HEADS * ATT_V
            tn = 512
            rope = _rope_tables(seq, LANES // ATT_QK, ATT_QK ** -0.5 * LOG2E) + (q_cols, k_cols)
            a_lat, r_lat = _proj(xl, mod3, lat_row(tm_p), g_l, mix_w_in, e, 3, 2, tm_p, tn, att_cols, rope=rope)
            a_ctx, r_ctx = _proj(xc, mod3, ctx_rowf, g_l, mix_w_in, e, 3, 2, tm_c, tn, att_cols)
            lat3 = lambda a: a.reshape(batch, seq, -1)
            ctx3 = lambda a: a.reshape(batch, ctx_len, -1)
            oa = _attention(lat3(a_lat), ctx3(a_ctx), diff_lambda[e], diff_norm_g[e][None, :], lam_init,
                            min(256, seq // 4))
            orec = _hgrn(lat3(r_lat), ctx3(r_ctx), rec_lb, rec_norm_g[e][None, :], e)
            w_bf = mix_w_out[e].astype(BF16)
            xl_new = _mix_out(oa.reshape(batch * seq, -1), orec.reshape(batch * seq, -1), w_bf, xl, mod3,
                              lat_row(tm_o), g_l, 3, 2, tm_o)
            if ctx_out:
                raise NotImplementedError("context mixer output is not needed for this depth")
        else:
            o = l // 2
            bg, cv = _conv_in(xl, mod3, lat_row(tm_p), g_l, conv_w_in, o, 3, 2, tm_p, 256)
            xl_new = _conv_out(bg, cv, conv_w[o], conv_w_out[o].astype(BF16), xl, mod3, lat_row(tm_o), g_l,
                               3, 2, tm_o, seq)
            if ctx_out:
                raise NotImplementedError("context mixer output is not needed for this depth")
        xl = xl_new
        xl = ffn(xl, lat_row(tm_l), 1, 6, 4, tm_l)
    return xl.reshape(batch, seq, d)
```

```python
import functools
import math

import jax
import jax.numpy as jnp
from jax import lax
from jax.experimental import pallas as pl
from jax.experimental.pallas import tpu as pltpu

F32 = jnp.float32
BF16 = jnp.bfloat16

EPS = 1e-6
N_MOD = 9
FFN_RES_WEIGHT = 0.5
GRID_W = 64
ATT_HEADS = 8
ATT_QK = 64
ATT_V = 2 * ATT_QK
ROPE_THETA = 10000.0
ROPE_PAIRS = ATT_QK // 4
LAMBDA_INIT_BASE = 0.8
LAMBDA_INIT_AMP = 0.6
LAMBDA_INIT_RATE = 0.3
REC_HEADS = 8
REC_DK = 128
REC_DV = 128
REC_CHUNK = 64
CONV_K = 3

LANES = 128
MIB = 1024 * 1024
MOD_ROWS = 8
LOG2E = 1.4426950408889634


def _params(sem, vmem_mib, flags=None):
    return pltpu.CompilerParams(dimension_semantics=sem, vmem_limit_bytes=vmem_mib * MIB, flags=flags)


def _rms(y, g):
    return y * lax.rsqrt(jnp.mean(y * y, axis=-1, keepdims=True) + EPS) * g


def _dot(a, b):
    return jnp.dot(a, b, preferred_element_type=F32)


def _dot_nt(a, b):
    return lax.dot_general(a, b, (((1,), (1,)), ((), ())), preferred_element_type=F32)


def _dot_tn(a, b):
    return lax.dot_general(a, b, (((0,), (0,)), ((), ())), preferred_element_type=F32)


def _mod_kernel(c_ref, w_ref, b_ref, o_ref):
    cv = c_ref[...]
    s = (cv * jax.nn.sigmoid(cv)).astype(BF16)
    o_ref[...] = _dot(s, w_ref[...].astype(BF16)) + b_ref[...]


def _modulation(cvec, ada_w, ada_b, layer):
    depth, d, n = ada_w.shape
    tn = 1024
    b3 = ada_b.reshape(depth, 1, n)
    out = pl.pallas_call(
        _mod_kernel,
        grid=(n // tn,),
        in_specs=[
            pl.BlockSpec((MOD_ROWS, d), lambda j: (0, 0)),
            pl.BlockSpec((None, d, tn), lambda j: (layer, 0, j)),
            pl.BlockSpec((None, 1, tn), lambda j: (layer, 0, j)),
        ],
        out_specs=pl.BlockSpec((MOD_ROWS, tn), lambda j: (0, j)),
        out_shape=jax.ShapeDtypeStruct((MOD_ROWS, n), F32),
        compiler_params=_params(("arbitrary",), 40),
        name="modulation",
    )(cvec, ada_w, b3)
    return out.reshape(MOD_ROWS, N_MOD, d)


def _ffn_kernel(x_ref, mod_ref, g_ref, wg_ref, wu_ref, wd_ref, o_ref, h_ref, *, k0, gi, nj, tf, tail, rc):
    j = pl.program_id(1)
    row_chunks = [slice(r, r + rc) for r in range(0, x_ref.shape[0], rc)]

    def weights(width):
        w_gu = jnp.concatenate([wg_ref[:, :width].astype(BF16), wu_ref[:, :width].astype(BF16)], axis=1)
        return w_gu, wd_ref[:width, :].astype(BF16)

    def partial_down(h, w):
        gu = _dot(h, w[0])
        width = gu.shape[-1] // 2
        g, u = gu[:, :width], gu[:, width:]
        a = (g * jax.nn.sigmoid(g) * u).astype(BF16)
        return _dot(a, w[1])

    @pl.when(j == 0)
    def _():
        w = weights(tf)
        for rows in row_chunks:
            h = _rms(x_ref[rows, :], g_ref[gi:gi + 1, :]) * (1.0 + mod_ref[k0 + 1:k0 + 2, :]) + mod_ref[k0:k0 + 1, :]
            h = h.astype(BF16)
            h_ref[rows, :] = h
            o_ref[rows, :] = partial_down(h, w)

    @pl.when((j > 0) & (j < nj - 1))
    def _():
        o_ref[...] += partial_down(h_ref[...], weights(tf))

    @pl.when(j == nj - 1)
    def _():
        w = weights(tail)
        gate = FFN_RES_WEIGHT * mod_ref[k0 + 2:k0 + 3, :]
        for rows in row_chunks:
            y = o_ref[rows, :] + partial_down(h_ref[rows, :], w)
            o_ref[rows, :] = x_ref[rows, :] + gate * _rms(y, g_ref[gi + 1:gi + 2, :])


def _ffn_half(x, mod3, row_fn, g_l, wg, wu, wd, layer, half, k0, gi, tm, tf=256):
    t, d = x.shape
    f = wg.shape[-1]
    nj = pl.cdiv(f, tf)
    tail = f - (nj - 1) * tf
    kern = functools.partial(_ffn_kernel, k0=k0, gi=gi, nj=nj, tf=tf, tail=tail, rc=min(tm, 256))
    return pl.pallas_call(
        kern,
        grid=(t // tm, nj),
        in_specs=[
            pl.BlockSpec((tm, d), lambda i, j: (i, 0)),
            pl.BlockSpec((None, N_MOD, d), lambda i, j: (row_fn(i), 0, 0)),
            pl.BlockSpec(g_l.shape, lambda i, j: (0, 0)),
            pl.BlockSpec((None, None, d, tf), lambda i, j: (layer, half, 0, j)),
            pl.BlockSpec((None, None, d, tf), lambda i, j: (layer, half, 0, j)),
            pl.BlockSpec((None, None, tf, d), lambda i, j: (layer, half, j, 0)),
        ],
        out_specs=pl.BlockSpec((tm, d), lambda i, j: (i, 0)),
        out_shape=jax.ShapeDtypeStruct((t, d), F32),
        scratch_shapes=[pltpu.VMEM((tm, d), BF16)],
        compiler_params=_params(("parallel", "arbitrary"), 56),
        name="ffn_half",
    )(x, mod3, g_l, wg, wu, wd)


def _proj_kernel(*refs, k0, gi, n_lo, rope_blocks, rc):
    if rope_blocks:
        x_ref, mod_ref, g_ref, w_ref, cos_ref, sin_ref, lo_ref, hi_ref, h_ref = refs
    else:
        x_ref, mod_ref, g_ref, w_ref, lo_ref, hi_ref, h_ref = refs
    j = pl.program_id(1)
    tn = w_ref.shape[-1]
    assert n_lo >= 1
    row_chunks = [slice(r, r + rc) for r in range(0, x_ref.shape[0], rc)]

    def rotate(p, rows):
        lane = lax.broadcasted_iota(jnp.int32, (1, tn), 1)
        first = (lane % (2 * ROPE_PAIRS)) < ROPE_PAIRS
        partner = jnp.where(first, pltpu.roll(p, tn - ROPE_PAIRS, axis=1), pltpu.roll(p, ROPE_PAIRS, axis=1))
        reps = (1, tn // LANES)
        return p * jnp.tile(cos_ref[rows, :], reps) + partner * jnp.tile(sin_ref[rows, :], reps)

    @pl.when(j == 0)
    def _():
        w = w_ref[...].astype(BF16)
        for rows in row_chunks:
            h = _rms(x_ref[rows, :], g_ref[gi:gi + 1, :]) * (1.0 + mod_ref[k0 + 1:k0 + 2, :]) + mod_ref[k0:k0 + 1, :]
            h = h.astype(BF16)
            h_ref[rows, :] = h
            p = _dot(h, w)
            lo_ref[rows, :] = (rotate(p, rows) if rope_blocks else p).astype(BF16)

    if rope_blocks > 1:
        @pl.when((j > 0) & (j < rope_blocks))
        def _():
            w = w_ref[...].astype(BF16)
            for rows in row_chunks:
                lo_ref[rows, :] = rotate(_dot(h_ref[rows, :], w), rows).astype(BF16)

    @pl.when((j >= max(rope_blocks, 1)) & (j < n_lo))
    def _():
        lo_ref[...] = _dot(h_ref[...], w_ref[...].astype(BF16)).astype(BF16)

    @pl.when(j >= n_lo)
    def _():
        hi_ref[...] = _dot(h_ref[...], w_ref[...].astype(BF16))


def _proj(x, mod3, row_fn, g_l, w, w_idx, k0, gi, tm, tn, lo_cols, rope=None):
    t, d = x.shape
    n = w.shape[-1]
    n_lo = lo_cols // tn
    in_specs = [
        pl.BlockSpec((tm, d), lambda i, j: (i, 0), pipeline_mode=pl.Buffered(1 if tm > 1024 else 2)),
        pl.BlockSpec((None, N_MOD, d), lambda i, j: (row_fn(i), 0, 0)),
        pl.BlockSpec(g_l.shape, lambda i, j: (0, 0)),
        pl.BlockSpec((None, d, tn), lambda i, j: (w_idx, 0, j)),
    ]
    args = [x, mod3, g_l, w]
    rope_blocks = 0
    if rope is not None:
        cos_t, sin_t, q_cols, k_cols = rope
        seq_tiles = cos_t.shape[1] // tm
        q_blocks = q_cols // tn
        rope_blocks = (q_cols + k_cols) // tn
        in_specs += [pl.BlockSpec((None, tm, LANES),
                                  lambda i, j: (jnp.minimum(j // q_blocks, 1), i % seq_tiles, 0))] * 2
        args += [cos_t, sin_t]
    kern = functools.partial(_proj_kernel, k0=k0, gi=gi, n_lo=n_lo, rope_blocks=rope_blocks, rc=min(tm, 256))
    return pl.pallas_call(
        kern,
        grid=(t // tm, n // tn),
        in_specs=in_specs,
        out_specs=[pl.BlockSpec((tm, tn), lambda i, j: (i, jnp.minimum(j, n_lo - 1))),
                   pl.BlockSpec((tm, tn), lambda i, j: (i, jnp.maximum(j - n_lo, 0)))],
        out_shape=[jax.ShapeDtypeStruct((t, lo_cols), BF16), jax.ShapeDtypeStruct((t, n - lo_cols), F32)],
        scratch_shapes=[pltpu.VMEM((tm, d), BF16)],
        compiler_params=_params(("parallel", "arbitrary"), 60),
        name="norm_proj",
    )(*args)


def _rope_tables(seq, reps, q_scale):
    n = jnp.arange(seq)
    row = (n // GRID_W).astype(F32)
    col = (n % GRID_W).astype(F32)
    freqs = ROPE_THETA ** (-jnp.arange(ROPE_PAIRS, dtype=F32) / ROPE_PAIRS)

    def half(pos):
        ang = pos[:, None] * freqs
        c, s = jnp.cos(ang), jnp.sin(ang)
        return jnp.concatenate([c, c], axis=-1), jnp.concatenate([-s, s], axis=-1)

    cr, sr = half(row)
    cc, sc = half(col)
    cos_t = jnp.tile(jnp.concatenate([cr, cc], axis=-1), (1, reps))
    sin_t = jnp.tile(jnp.concatenate([sr, sc], axis=-1), (1, reps))
    return jnp.stack([cos_t * q_scale, cos_t]), jnp.stack([sin_t * q_scale, sin_t])


def _attn_kernel(q_ref, kl_ref, vl_ref, kc_ref, vc_ref, lv_ref, gn_ref, o_ref,
                 k_scr, v_scr, s0_scr, s1_scr, w0_scr, w1_scr, *, lam_init, tq, heads):
    n = q_ref.shape[0]
    nc = kc_ref.shape[0]
    tiles = [(h, r) for h in range(heads) for r in range(0, n, tq)]
    assert len(tiles) >= 3
    for h in range(heads):
        cols = slice(h * LANES, (h + 1) * LANES)
        k_scr[h, :nc, :] = kc_ref[:, cols]
        k_scr[h, nc:, :] = kl_ref[:, cols]
        v_scr[h, :nc, :] = vc_ref[:, cols]
        v_scr[h, nc:, :] = vl_ref[:, cols]
    lane = lax.broadcasted_iota(jnp.int32, (1, 2 * ATT_QK), 1)
    lv = lv_ref[...]
    lam = (jnp.exp(jnp.sum(lv[0:1] * lv[1:2], axis=-1, keepdims=True))
           - jnp.exp(jnp.sum(lv[2:3] * lv[3:4], axis=-1, keepdims=True)) + lam_init)
    s_scr = (s0_scr, s1_scr)
    w_scr = (w0_scr, w1_scr)

    def scores(t):
        h, r = tiles[t]
        q = q_ref[r:r + tq, h * LANES:(h + 1) * LANES]
        zero = jnp.zeros_like(q)
        k = k_scr[h]
        s_scr[t % 2][0] = _dot_nt(jnp.where(lane < ATT_QK, q, zero), k)
        s_scr[t % 2][1] = _dot_nt(jnp.where(lane >= ATT_QK, q, zero), k)

    def weights(t):
        def numerators(m):
            s = s_scr[t % 2][m]
            e = jnp.exp2(s - jnp.max(s, axis=-1, keepdims=True))
            return e, 1.0 / jnp.sum(e, axis=-1, keepdims=True)

        e1, inv1 = numerators(0)
        e2, inv2 = numerators(1)
        w_scr[t % 2][...] = (e1 * inv1 - e2 * (lam * inv2)).astype(BF16)

    def outputs(t):
        h, r = tiles[t]
        o = _dot(w_scr[t % 2][...], v_scr[h])
        o_ref[r:r + tq, h * LANES:(h + 1) * LANES] = (_rms(o, gn_ref[...]) * (1.0 - lam_init)).astype(o_ref.dtype)

    last = len(tiles) - 1
    scores(0)
    scores(1)
    weights(0)
    for t in range(2, last + 1):
        scores(t)
        weights(t - 1)
        outputs(t - 2)
    weights(last)
    outputs(last - 1)
    outputs(last)


def _attention(a_lat, a_ctx, lam_vecs, norm_g, lam_init, tq, heads):
    b, n, _ = a_lat.shape
    nc = a_ctx.shape[1]
    width = heads * LANES
    kb = ATT_HEADS // heads
    vb = 2 * ATT_HEADS // heads
    kern = functools.partial(_attn_kernel, lam_init=lam_init, tq=tq, heads=heads)
    nk = n + nc
    return pl.pallas_call(
        kern,
        grid=(b, ATT_HEADS // heads),
        in_specs=[
            pl.BlockSpec((None, n, width), lambda bi, h: (bi, 0, h)),
            pl.BlockSpec((None, n, width), lambda bi, h: (bi, 0, kb + h)),
            pl.BlockSpec((None, n, width), lambda bi, h: (bi, 0, vb + h)),
            pl.BlockSpec((None, nc, width), lambda bi, h: (bi, 0, kb + h)),
            pl.BlockSpec((None, nc, width), lambda bi, h: (bi, 0, vb + h)),
            pl.BlockSpec(lam_vecs.shape, lambda bi, h: (0, 0)),
            pl.BlockSpec(norm_g.shape, lambda bi, h: (0, 0)),
        ],
        out_specs=pl.BlockSpec((None, n, width), lambda bi, h: (bi, 0, h)),
        out_shape=jax.ShapeDtypeStruct((b, n, ATT_HEADS * ATT_V), BF16),
        scratch_shapes=[
            pltpu.VMEM((heads, nk, LANES), BF16),
            pltpu.VMEM((heads, nk, LANES), BF16),
            pltpu.VMEM((2, tq, nk), F32),
            pltpu.VMEM((2, tq, nk), F32),
            pltpu.VMEM((tq, nk), BF16),
            pltpu.VMEM((tq, nk), BF16),
        ],
        compiler_params=_params(("parallel", "parallel"), 56),
        name="diff_attention",
    )(a_lat, a_lat, a_lat, a_ctx, a_ctx, lam_vecs, norm_g)


def _hgrn_kernel(q_ref, zf_ref, zb_ref, i_ref, g_ref, czf_ref, czb_ref, ci_ref, lb_ref, gn_ref, o_ref,
                 o_scr, qs_scr, u_scr, dl_scr, *, n_lat, n_ctx, slot, rb):
    C = REC_CHUNK
    nct = n_ctx // C
    ncl = n_lat // C
    cpb = rb // C

    def lower_bound(d):
        a = lb_ref[d]
        ex = jnp.exp(a - jnp.max(a, axis=0, keepdims=True))
        return jnp.sum(ex[:slot + 1], axis=0, keepdims=True) / jnp.sum(ex, axis=0, keepdims=True)

    lb = (lower_bound(0), lower_bound(1))
    r_i = lax.broadcasted_iota(jnp.int32, (rb, rb), 0)
    c_i = lax.broadcasted_iota(jnp.int32, (rb, rb), 1)
    same = (r_i // C) == (c_i // C)
    masks = (same & (c_i <= r_i), same & (c_i >= r_i))

    tri = masks[0].astype(BF16)

    def chunk_rows(a, offset):
        return jnp.concatenate(
            [jnp.broadcast_to(a[c * C + offset:c * C + offset + 1, :], (C, LANES)) for c in range(cpb)], axis=0)

    def gate(z, lower):
        f = lower + (1.0 - lower) * jax.nn.sigmoid(z)
        return 1.0 - f, jnp.log(f)

    def prep(zf, zb, v, q, base, blk, rows):
        kk_f, lg_f = gate(zf, lb[0])
        kk_b, lg_b = gate(zb, lb[1])
        lg = jnp.concatenate([lg_f, lg_b], axis=1)
        hi = lg.astype(BF16)
        lo = (lg - hi.astype(F32)).astype(BF16)
        pre = _dot(tri, hi) + _dot(tri, lo)
        cum_f = pre[:, :LANES]
        last_f = chunk_rows(cum_f, C - 1)
        last_b = chunk_rows(pre[:, LANES:], C - 1)
        cum_b = last_b - pre[:, LANES:] + lg_b
        kd = jnp.concatenate([kk_f * jnp.exp(last_f - cum_f), kk_b * jnp.exp(last_b - cum_b)], axis=1).astype(BF16)
        for c in range(cpb):
            idx = base + blk * cpb + c
            u = _dot_tn(v[c * C:(c + 1) * C], kd[c * C:(c + 1) * C])
            u_scr[0, idx] = u[:, :LANES]
            u_scr[1, idx] = u[:, LANES:]
            dl_scr[0, idx] = jnp.exp(last_f[c * C:c * C + 8, :])
            dl_scr[1, idx] = jnp.exp(last_b[c * C:c * C + 8, :])
        if q is None:
            return None

        def intra(d, cum, kk, mid):
            ref = chunk_rows(cum, mid)
            dlt = cum - ref
            qe = q * jnp.exp(dlt)
            ke = kk * jnp.exp(-dlt)
            qs_scr[d, rows, :] = (qe * jnp.exp(ref)).astype(BF16)
            return jnp.where(masks[d], _dot_nt(qe.astype(BF16), ke.astype(BF16)), 0.0)

        att = intra(0, cum_f, kk_f, C // 2 - 1) + intra(1, cum_b, kk_b, C // 2)
        return _dot(att.astype(BF16), v)

    for blk in range(n_ctx // rb):
        rows = slice(blk * rb, (blk + 1) * rb)
        prep(czf_ref[rows, :], czb_ref[rows, :], ci_ref[rows, :].astype(BF16), None, 0, blk, rows)

    def lat_block(blk, carry):
        rows = pl.ds(pl.multiple_of(blk * rb, rb), rb)
        v = i_ref[rows, :].astype(BF16)
        q = q_ref[rows, :] * (REC_DK ** -0.5)
        o_scr[rows, :] = prep(zf_ref[rows, :], zb_ref[rows, :], v, q, nct, blk, rows)
        return carry

    lax.fori_loop(0, n_lat // rb, lat_block, 0, unroll=4)

    def advance(d, idx, st):
        return st * dl_scr[d, idx, 0:1, :] + u_scr[d, idx]

    st_f = jnp.zeros((REC_DV, REC_DK), F32)
    st_b = jnp.zeros((REC_DV, REC_DK), F32)
    for c in range(nct):
        st_f = advance(0, c, st_f)
        st_b = advance(1, nct - 1 - c, st_b)

    def seq_step(s, carry):
        st_f, st_b = carry
        cf = s
        cb = ncl - 1 - s
        rows_f = pl.ds(pl.multiple_of(cf * C, C), C)
        rows_b = pl.ds(pl.multiple_of(cb * C, C), C)
        o_scr[rows_f, :] += _dot_nt(qs_scr[0, rows_f, :], st_f.astype(BF16))
        o_scr[rows_b, :] += _dot_nt(qs_scr[1, rows_b, :], st_b.astype(BF16))
        return advance(0, nct + cf, st_f), advance(1, nct + cb, st_b)

    lax.fori_loop(0, ncl, seq_step, (st_f, st_b), unroll=16)

    g = g_ref[...]
    o_ref[...] = (_rms(o_scr[...], gn_ref[...]) * (g * jax.nn.sigmoid(g))).astype(o_ref.dtype)


def _hgrn(r_lat, r_ctx, rec_lb, norm_g, slot):
    b, n, _ = r_lat.shape
    nc = r_ctx.shape[1]
    rb = math.gcd(256, math.gcd(n, nc))
    nchunks = (n + nc) // REC_CHUNK
    cols = [k * REC_HEADS for k in range(5)]
    lat_spec = lambda c: pl.BlockSpec((None, n, LANES), lambda bi, h: (bi, 0, c + h))
    ctx_spec = lambda c: pl.BlockSpec((None, nc, LANES), lambda bi, h: (bi, 0, c + h))
    kern = functools.partial(_hgrn_kernel, n_lat=n, n_ctx=nc, slot=slot, rb=rb)
    return pl.pallas_call(
        kern,
        grid=(b, REC_HEADS),
        in_specs=[lat_spec(cols[0]), lat_spec(cols[1]), lat_spec(cols[2]), lat_spec(cols[3]), lat_spec(cols[4]),
                  ctx_spec(cols[1]), ctx_spec(cols[2]), ctx_spec(cols[3]),
                  pl.BlockSpec((2, rec_lb.shape[1], LANES), lambda bi, h: (0, 0, h)),
                  pl.BlockSpec(norm_g.shape, lambda bi, h: (0, 0))],
        out_specs=pl.BlockSpec((None, n, LANES), lambda bi, h: (bi, 0, h)),
        out_shape=jax.ShapeDtypeStruct((b, n, REC_HEADS * REC_DV), BF16),
        scratch_shapes=[
            pltpu.VMEM((n, LANES), F32),
            pltpu.VMEM((2, n, LANES), BF16),
            pltpu.VMEM((2, nchunks, REC_DV, REC_DK), F32),
            pltpu.VMEM((2, nchunks, 8, LANES), F32),
        ],
        compiler_params=_params(("parallel", "parallel"), 48),
        name="hgrn2_bidir",
    )(r_lat, r_lat, r_lat, r_lat, r_lat, r_ctx, r_ctx, r_ctx, rec_lb, norm_g)


def _mix_out_kernel(a_ref, r_ref, w_ref, x_ref, mod_ref, g_ref, o_ref, *, k0, gi):
    ka = a_ref.shape[-1]
    y = _dot(a_ref[...], w_ref[:ka, :]) + _dot(r_ref[...], w_ref[ka:, :])
    o_ref[...] = x_ref[...] + mod_ref[k0 + 2:k0 + 3, :] * _rms(y, g_ref[gi + 1:gi + 2, :])


def _mix_out(oa, orec, w_bf, x, mod3, row_fn, g_l, k0, gi, tm):
    t, d = x.shape
    ka, kr = oa.shape[-1], orec.shape[-1]
    kern = functools.partial(_mix_out_kernel, k0=k0, gi=gi)
    return pl.pallas_call(
        kern,
        grid=(t // tm,),
        in_specs=[
            pl.BlockSpec((tm, ka), lambda i: (i, 0)),
            pl.BlockSpec((tm, kr), lambda i: (i, 0)),
            pl.BlockSpec(w_bf.shape, lambda i: (0, 0)),
            pl.BlockSpec((tm, d), lambda i: (i, 0)),
            pl.BlockSpec((None, N_MOD, d), lambda i: (row_fn(i), 0, 0)),
            pl.BlockSpec(g_l.shape, lambda i: (0, 0)),
        ],
        out_specs=pl.BlockSpec((tm, d), lambda i: (i, 0)),
        out_shape=jax.ShapeDtypeStruct((t, d), F32),
        compiler_params=_params(("parallel",), 48),
        name="mixer_out",
    )(oa, orec, w_bf, x, mod3, g_l)


def _conv_in_kernel(x_ref, mod_ref, g_ref, wb_ref, wc_ref, wv_ref, b_ref, cv_ref, h_ref, *, k0, gi, rc):
    j = pl.program_id(1)

    def weights():
        return wb_ref[...].astype(BF16), wc_ref[...].astype(BF16), wv_ref[...].astype(BF16)

    @pl.when(j == 0)
    def _():
        wb, wc, wv = weights()
        for r in range(0, x_ref.shape[0], rc):
            rows = slice(r, r + rc)
            h = _rms(x_ref[rows, :], g_ref[gi:gi + 1, :]) * (1.0 + mod_ref[k0 + 1:k0 + 2, :]) + mod_ref[k0:k0 + 1, :]
            h = h.astype(BF16)
            h_ref[rows, :] = h
            b_ref[rows, :] = _dot(h, wb)
            cv_ref[rows, :] = _dot(h, wc) * _dot(h, wv)

    @pl.when(j > 0)
    def _():
        wb, wc, wv = weights()
        h = h_ref[...]
        b_ref[...] = _dot(h, wb)
        cv_ref[...] = _dot(h, wc) * _dot(h, wv)


def _conv_in(x, mod3, row_fn, g_l, w, w_idx, k0, gi, tm, tn):
    t, d = x.shape
    width = w.shape[-1] // 3
    nb = width // tn
    kern = functools.partial(_conv_in_kernel, k0=k0, gi=gi, rc=min(tm, 256))
    w_spec = lambda off: pl.BlockSpec((None, d, tn), lambda i, j: (w_idx, 0, off + j))
    return pl.pallas_call(
        kern,
        grid=(t // tm, nb),
        in_specs=[
            pl.BlockSpec((tm, d), lambda i, j: (i, 0), pipeline_mode=pl.Buffered(1 if tm > 1024 else 2)),
            pl.BlockSpec((None, N_MOD, d), lambda i, j: (row_fn(i), 0, 0)),
            pl.BlockSpec(g_l.shape, lambda i, j: (0, 0)),
            w_spec(0), w_spec(nb), w_spec(2 * nb),
        ],
        out_specs=[pl.BlockSpec((tm, tn), lambda i, j: (i, j))] * 2,
        out_shape=[jax.ShapeDtypeStruct((t, width), F32)] * 2,
        scratch_shapes=[pltpu.VMEM((tm, d), BF16)],
        compiler_params=_params(("parallel", "arbitrary"), 60),
        name="conv_in",
    )(x, mod3, g_l, w, w, w)


def _conv_out_kernel(b_ref, cv_ref, prev_ref, next_ref, cw_ref, w_ref, x_ref, mod_ref, g_ref, o_ref,
                     *, k0, gi, seq_tiles):
    i = pl.program_id(0)
    tm = cv_ref.shape[0]
    cv = cv_ref[...]
    row = lax.broadcasted_iota(jnp.int32, (tm, 1), 0)
    prev = jnp.where(i % seq_tiles == 0, 0.0, prev_ref[7:8, :])
    nxt = jnp.where(i % seq_tiles == seq_tiles - 1, 0.0, next_ref[0:1, :])
    up = jnp.where(row == 0, prev, pltpu.roll(cv, 1, axis=0))
    dn = jnp.where(row == tm - 1, nxt, pltpu.roll(cv, tm - 1, axis=0))
    u = cw_ref[0:1, :] * up + cw_ref[1:2, :] * cv + cw_ref[2:3, :] * dn
    y = _dot((b_ref[...] * u).astype(BF16), w_ref[...])
    o_ref[...] = x_ref[...] + mod_ref[k0 + 2:k0 + 3, :] * _rms(y, g_ref[gi + 1:gi + 2, :])


def _conv_out(bg, cv, conv_w, w_bf, x, mod3, row_fn, g_l, k0, gi, tm, seq):
    t, d = x.shape
    width = cv.shape[-1]
    halo = 8
    hb = tm // halo
    last = t // halo - 1
    kern = functools.partial(_conv_out_kernel, k0=k0, gi=gi, seq_tiles=seq // tm)
    return pl.pallas_call(
        kern,
        grid=(t // tm,),
        in_specs=[
            pl.BlockSpec((tm, width), lambda i: (i, 0)),
            pl.BlockSpec((tm, width), lambda i: (i, 0)),
            pl.BlockSpec((halo, width), lambda i: (jnp.maximum(i * hb - 1, 0), 0)),
            pl.BlockSpec((halo, width), lambda i: (jnp.minimum((i + 1) * hb, last), 0)),
            pl.BlockSpec(conv_w.shape, lambda i: (0, 0)),
            pl.BlockSpec(w_bf.shape, lambda i: (0, 0)),
            pl.BlockSpec((tm, d), lambda i: (i, 0)),
            pl.BlockSpec((None, N_MOD, d), lambda i: (row_fn(i), 0, 0)),
            pl.BlockSpec(g_l.shape, lambda i: (0, 0)),
        ],
        out_specs=pl.BlockSpec((tm, d), lambda i: (i, 0)),
        out_shape=jax.ShapeDtypeStruct((t, d), F32),
        compiler_params=_params(("parallel",), 48),
        name="conv_out",
    )(bg, cv, cv, cv, conv_w, w_bf, x, mod3, g_l)


def _tile(n, pref):
    t = pref
    while n % t:
        t //= 2
    return t


def kernel(x, c, ctx, c_ctx, ada_w, ada_b, norm_g, ffn_w_gate, ffn_w_up, ffn_w_down, mix_w_in, mix_w_out,
           diff_lambda, diff_norm_g, rec_norm_g, rec_lb, conv_w_in, conv_w, conv_w_out):
    batch, seq, d = x.shape
    ctx_len = ctx.shape[1]
    depth = ada_w.shape[0]
    ctx_row = batch
    assert batch < MOD_ROWS

    cvec = jnp.concatenate([c, c_ctx[None, :], jnp.zeros((MOD_ROWS - batch - 1, d), F32)], axis=0)
    xl = x.reshape(batch * seq, d)
    xc = ctx.reshape(batch * ctx_len, d)

    tm_l = _tile(seq, 1024)
    tm_c = _tile(batch * ctx_len, 1024)
    lat_row = lambda tm: (lambda i: i // (seq // tm))
    ctx_rowf = lambda i: ctx_row

    for l in range(depth):
        even = l % 2 == 0
        ctx_out = any(j % 2 == 0 for j in range(l + 1, depth))
        ctx_in = even or ctx_out
        g_l = norm_g[l]
        mod3 = _modulation(cvec, ada_w, ada_b, l)
        ffn = lambda xx, row_fn, half, k0, gi, tm: _ffn_half(
            xx, mod3, row_fn, g_l, ffn_w_gate, ffn_w_up, ffn_w_down, l, half, k0, gi, tm)

        xl = ffn(xl, lat_row(tm_l), 0, 0, 0, tm_l)
        if ctx_in:
            xc = ffn(xc, ctx_rowf, 0, 0, 0, tm_c)

        tm_o = _tile(seq, 512)
        tm_p = _tile(seq, 2048)
        if even:
            e = l // 2
            lam_init = LAMBDA_INIT_BASE - LAMBDA_INIT_AMP * math.exp(-LAMBDA_INIT_RATE * l)
            q_cols = k_cols = ATT_HEADS * 2 * ATT_QK
            att_cols = q_cols + k_cols + ATT_HEADS * ATT_V
            tn = 512
            rope = _rope_tables(seq, LANES // ATT_QK, ATT_QK ** -0.5 * LOG2E) + (q_cols, k_cols)
            a_lat, r_lat = _proj(xl, mod3, lat_row(tm_p), g_l, mix_w_in, e, 3, 2, tm_p, tn, att_cols, rope=rope)
            a_ctx, r_ctx = _proj(xc, mod3, ctx_rowf, g_l, mix_w_in, e, 3, 2, tm_c, tn, att_cols)
            lat3 = lambda a: a.reshape(batch, seq, -1)
            ctx3 = lambda a: a.reshape(batch, ctx_len, -1)
            oa = _attention(lat3(a_lat), ctx3(a_ctx), diff_lambda[e], diff_norm_g[e][None, :], lam_init,
                            _tile(seq, 512), 2)
            orec = _hgrn(lat3(r_lat), ctx3(r_ctx), rec_lb, rec_norm_g[e][None, :], e)
            w_bf = mix_w_out[e].astype(BF16)
            xl_new = _mix_out(oa.reshape(batch * seq, -1), orec.reshape(batch * seq, -1), w_bf, xl, mod3,
                              lat_row(tm_o), g_l, 3, 2, tm_o)
            if ctx_out:
                raise NotImplementedError("context mixer output is not needed for this depth")
        else:
            o = l // 2
            bg, cv = _conv_in(xl, mod3, lat_row(tm_p), g_l, conv_w_in, o, 3, 2, tm_p, 256)
            xl_new = _conv_out(bg, cv, conv_w[o], conv_w_out[o].astype(BF16), xl, mod3, lat_row(tm_o), g_l,
                               3, 2, tm_o, seq)
            if ctx_out:
                raise NotImplementedError("context mixer output is not needed for this depth")
        xl = xl_new
        xl = ffn(xl, lat_row(tm_l), 1, 6, 4, tm_l)
    return xl.reshape(batch, seq, d)
```

```python
import functools
import math

import jax
import jax.numpy as jnp
from jax import lax
from jax.experimental import pallas as pl
from jax.experimental.pallas import tpu as pltpu

F32 = jnp.float32
BF16 = jnp.bfloat16

EPS = 1e-6
N_MOD = 9
FFN_RES_WEIGHT = 0.5
GRID_W = 64
ATT_HEADS = 8
ATT_QK = 64
ATT_V = 2 * ATT_QK
ROPE_THETA = 10000.0
ROPE_PAIRS = ATT_QK // 4
LAMBDA_INIT_BASE = 0.8
LAMBDA_INIT_AMP = 0.6
LAMBDA_INIT_RATE = 0.3
REC_HEADS = 8
REC_DK = 128
REC_DV = 128
REC_CHUNK = 64
CONV_K = 3

LANES = 128
MIB = 1024 * 1024
MOD_ROWS = 8
LOG2E = 1.4426950408889634


def _params(sem, vmem_mib, flags=None):
    return pltpu.CompilerParams(dimension_semantics=sem, vmem_limit_bytes=vmem_mib * MIB, flags=flags)


def _rms(y, g):
    return y * lax.rsqrt(jnp.mean(y * y, axis=-1, keepdims=True) + EPS) * g


def _dot(a, b):
    return jnp.dot(a, b, preferred_element_type=F32)


def _dot_nt(a, b):
    return lax.dot_general(a, b, (((1,), (1,)), ((), ())), preferred_element_type=F32)


def _dot_tn(a, b):
    return lax.dot_general(a, b, (((0,), (0,)), ((), ())), preferred_element_type=F32)


def _mod_kernel(c_ref, w_ref, b_ref, o_ref):
    cv = c_ref[...]
    s = (cv * jax.nn.sigmoid(cv)).astype(BF16)
    o_ref[...] = _dot(s, w_ref[...].astype(BF16)) + b_ref[...]


def _modulation(cvec, ada_w, ada_b, layer):
    depth, d, n = ada_w.shape
    tn = 1024
    b3 = ada_b.reshape(depth, 1, n)
    out = pl.pallas_call(
        _mod_kernel,
        grid=(n // tn,),
        in_specs=[
            pl.BlockSpec((MOD_ROWS, d), lambda j: (0, 0)),
            pl.BlockSpec((None, d, tn), lambda j: (layer, 0, j)),
            pl.BlockSpec((None, 1, tn), lambda j: (layer, 0, j)),
        ],
        out_specs=pl.BlockSpec((MOD_ROWS, tn), lambda j: (0, j)),
        out_shape=jax.ShapeDtypeStruct((MOD_ROWS, n), F32),
        compiler_params=_params(("arbitrary",), 40),
        name="modulation",
    )(cvec, ada_w, b3)
    return out.reshape(MOD_ROWS, N_MOD, d)


def _ffn_kernel(x_ref, mod_ref, g_ref, wg_ref, wu_ref, wd_ref, o_ref, h_ref, *, k0, gi, nj, tf, tail, rc):
    j = pl.program_id(1)
    row_chunks = [slice(r, r + rc) for r in range(0, x_ref.shape[0], rc)]

    def weights(width):
        w_gu = jnp.concatenate([wg_ref[:, :width].astype(BF16), wu_ref[:, :width].astype(BF16)], axis=1)
        return w_gu, wd_ref[:width, :].astype(BF16)

    def partial_down(h, w):
        gu = _dot(h, w[0])
        width = gu.shape[-1] // 2
        g, u = gu[:, :width], gu[:, width:]
        a = (g * jax.nn.sigmoid(g) * u).astype(BF16)
        return _dot(a, w[1])

    @pl.when(j == 0)
    def _():
        w = weights(tf)
        for rows in row_chunks:
            h = _rms(x_ref[rows, :], g_ref[gi:gi + 1, :]) * (1.0 + mod_ref[k0 + 1:k0 + 2, :]) + mod_ref[k0:k0 + 1, :]
            h = h.astype(BF16)
            h_ref[rows, :] = h
            o_ref[rows, :] = partial_down(h, w)

    @pl.when((j > 0) & (j < nj - 1))
    def _():
        o_ref[...] += partial_down(h_ref[...], weights(tf))

    @pl.when(j == nj - 1)
    def _():
        w = weights(tail)
        gate = FFN_RES_WEIGHT * mod_ref[k0 + 2:k0 + 3, :]
        for rows in row_chunks:
            y = o_ref[rows, :] + partial_down(h_ref[rows, :], w)
            o_ref[rows, :] = x_ref[rows, :] + gate * _rms(y, g_ref[gi + 1:gi + 2, :])


def _ffn_half(x, mod3, row_fn, g_l, wg, wu, wd, layer, half, k0, gi, tm, tf=256):
    t, d = x.shape
    f = wg.shape[-1]
    nj = pl.cdiv(f, tf)
    tail = f - (nj - 1) * tf
    kern = functools.partial(_ffn_kernel, k0=k0, gi=gi, nj=nj, tf=tf, tail=tail, rc=min(tm, 256))
    return pl.pallas_call(
        kern,
        grid=(t // tm, nj),
        in_specs=[
            pl.BlockSpec((tm, d), lambda i, j: (i, 0)),
            pl.BlockSpec((None, N_MOD, d), lambda i, j: (row_fn(i), 0, 0)),
            pl.BlockSpec(g_l.shape, lambda i, j: (0, 0)),
            pl.BlockSpec((None, None, d, tf), lambda i, j: (layer, half, 0, j)),
            pl.BlockSpec((None, None, d, tf), lambda i, j: (layer, half, 0, j)),
            pl.BlockSpec((None, None, tf, d), lambda i, j: (layer, half, j, 0)),
        ],
        out_specs=pl.BlockSpec((tm, d), lambda i, j: (i, 0)),
        out_shape=jax.ShapeDtypeStruct((t, d), F32),
        scratch_shapes=[pltpu.VMEM((tm, d), BF16)],
        compiler_params=_params(("parallel", "arbitrary"), 56),
        name="ffn_half",
    )(x, mod3, g_l, wg, wu, wd)


def _proj_kernel(*refs, k0, gi, n_lo, rope_blocks, rc):
    if rope_blocks:
        x_ref, mod_ref, g_ref, w_ref, cos_ref, sin_ref, lo_ref, hi_ref, h_ref = refs
    else:
        x_ref, mod_ref, g_ref, w_ref, lo_ref, hi_ref, h_ref = refs
    j = pl.program_id(1)
    tn = w_ref.shape[-1]
    assert n_lo >= 1
    row_chunks = [slice(r, r + rc) for r in range(0, x_ref.shape[0], rc)]

    def rotate(p, rows):
        lane = lax.broadcasted_iota(jnp.int32, (1, tn), 1)
        first = (lane % (2 * ROPE_PAIRS)) < ROPE_PAIRS
        partner = jnp.where(first, pltpu.roll(p, tn - ROPE_PAIRS, axis=1), pltpu.roll(p, ROPE_PAIRS, axis=1))
        reps = (1, tn // LANES)
        return p * jnp.tile(cos_ref[rows, :], reps) + partner * jnp.tile(sin_ref[rows, :], reps)

    @pl.when(j == 0)
    def _():
        w = w_ref[...].astype(BF16)
        for rows in row_chunks:
            h = _rms(x_ref[rows, :], g_ref[gi:gi + 1, :]) * (1.0 + mod_ref[k0 + 1:k0 + 2, :]) + mod_ref[k0:k0 + 1, :]
            h = h.astype(BF16)
            h_ref[rows, :] = h
            p = _dot(h, w)
            lo_ref[rows, :] = (rotate(p, rows) if rope_blocks else p).astype(BF16)

    if rope_blocks > 1:
        @pl.when((j > 0) & (j < rope_blocks))
        def _():
            w = w_ref[...].astype(BF16)
            for rows in row_chunks:
                lo_ref[rows, :] = rotate(_dot(h_ref[rows, :], w), rows).astype(BF16)

    @pl.when((j >= max(rope_blocks, 1)) & (j < n_lo))
    def _():
        lo_ref[...] = _dot(h_ref[...], w_ref[...].astype(BF16)).astype(BF16)

    @pl.when(j >= n_lo)
    def _():
        hi_ref[...] = _dot(h_ref[...], w_ref[...].astype(BF16))


def _proj(x, mod3, row_fn, g_l, w, w_idx, k0, gi, tm, tn, lo_cols, rope=None):
    t, d = x.shape
    n = w.shape[-1]
    n_lo = lo_cols // tn
    in_specs = [
        pl.BlockSpec((tm, d), lambda i, j: (i, 0), pipeline_mode=pl.Buffered(1 if tm > 1024 else 2)),
        pl.BlockSpec((None, N_MOD, d), lambda i, j: (row_fn(i), 0, 0)),
        pl.BlockSpec(g_l.shape, lambda i, j: (0, 0)),
        pl.BlockSpec((None, d, tn), lambda i, j: (w_idx, 0, j)),
    ]
    args = [x, mod3, g_l, w]
    rope_blocks = 0
    if rope is not None:
        cos_t, sin_t, q_cols, k_cols = rope
        seq_tiles = cos_t.shape[1] // tm
        q_blocks = q_cols // tn
        rope_blocks = (q_cols + k_cols) // tn
        in_specs += [pl.BlockSpec((None, tm, LANES),
                                  lambda i, j: (jnp.minimum(j // q_blocks, 1), i % seq_tiles, 0))] * 2
        args += [cos_t, sin_t]
    kern = functools.partial(_proj_kernel, k0=k0, gi=gi, n_lo=n_lo, rope_blocks=rope_blocks, rc=min(tm, 256))
    return pl.pallas_call(
        kern,
        grid=(t // tm, n // tn),
        in_specs=in_specs,
        out_specs=[pl.BlockSpec((tm, tn), lambda i, j: (i, jnp.minimum(j, n_lo - 1))),
                   pl.BlockSpec((tm, tn), lambda i, j: (i, jnp.maximum(j - n_lo, 0)))],
        out_shape=[jax.ShapeDtypeStruct((t, lo_cols), BF16), jax.ShapeDtypeStruct((t, n - lo_cols), F32)],
        scratch_shapes=[pltpu.VMEM((tm, d), BF16)],
        compiler_params=_params(("parallel", "arbitrary"), 60),
        name="norm_proj",
    )(*args)


def _rope_tables(seq, reps, q_scale):
    n = jnp.arange(seq)
    row = (n // GRID_W).astype(F32)
    col = (n % GRID_W).astype(F32)
    freqs = ROPE_THETA ** (-jnp.arange(ROPE_PAIRS, dtype=F32) / ROPE_PAIRS)

    def half(pos):
        ang = pos[:, None] * freqs
        c, s = jnp.cos(ang), jnp.sin(ang)
        return jnp.concatenate([c, c], axis=-1), jnp.concatenate([-s, s], axis=-1)

    cr, sr = half(row)
    cc, sc = half(col)
    cos_t = jnp.tile(jnp.concatenate([cr, cc], axis=-1), (1, reps))
    sin_t = jnp.tile(jnp.concatenate([sr, sc], axis=-1), (1, reps))
    return jnp.stack([cos_t * q_scale, cos_t]), jnp.stack([sin_t * q_scale, sin_t])


def _attn_kernel(q_ref, kl_ref, vl_ref, kc_ref, vc_ref, lv_ref, gn_ref, o_ref,
                 k_scr, v_scr, s0_scr, s1_scr, w0_scr, w1_scr, *, lam_init, tq, heads):
    n = q_ref.shape[0]
    nc = kc_ref.shape[0]
    tiles = [(h, r) for h in range(heads) for r in range(0, n, tq)]
    assert len(tiles) >= 3
    for h in range(heads):
        cols = slice(h * LANES, (h + 1) * LANES)
        k_scr[h, :nc, :] = kc_ref[:, cols]
        k_scr[h, nc:, :] = kl_ref[:, cols]
        v_scr[h, :nc, :] = vc_ref[:, cols]
        v_scr[h, nc:, :] = vl_ref[:, cols]
    lane = lax.broadcasted_iota(jnp.int32, (1, 2 * ATT_QK), 1)
    lv = lv_ref[...]
    lam = (jnp.exp(jnp.sum(lv[0:1] * lv[1:2], axis=-1, keepdims=True))
           - jnp.exp(jnp.sum(lv[2:3] * lv[3:4], axis=-1, keepdims=True)) + lam_init)
    s_scr = (s0_scr, s1_scr)
    w_scr = (w0_scr, w1_scr)

    def scores(t):
        h, r = tiles[t]
        q = q_ref[r:r + tq, h * LANES:(h + 1) * LANES]
        zero = jnp.zeros_like(q)
        k = k_scr[h]
        s_scr[t % 2][0] = _dot_nt(jnp.where(lane < ATT_QK, q, zero), k)
        s_scr[t % 2][1] = _dot_nt(jnp.where(lane >= ATT_QK, q, zero), k)

    def weights(t):
        def numerators(m):
            s = s_scr[t % 2][m]
            e = jnp.exp2(s - jnp.max(s, axis=-1, keepdims=True))
            return e, 1.0 / jnp.sum(e, axis=-1, keepdims=True)

        e1, inv1 = numerators(0)
        e2, inv2 = numerators(1)
        w_scr[t % 2][...] = (e1 * inv1 - e2 * (lam * inv2)).astype(BF16)

    def outputs(t):
        h, r = tiles[t]
        o = _dot(w_scr[t % 2][...], v_scr[h])
        o_ref[r:r + tq, h * LANES:(h + 1) * LANES] = (_rms(o, gn_ref[...]) * (1.0 - lam_init)).astype(o_ref.dtype)

    last = len(tiles) - 1
    scores(0)
    scores(1)
    weights(0)
    for t in range(2, last + 1):
        scores(t)
        weights(t - 1)
        outputs(t - 2)
    weights(last)
    outputs(last - 1)
    outputs(last)


def _attention(a_lat, a_ctx, lam_vecs, norm_g, lam_init, tq, heads):
    b, n, _ = a_lat.shape
    nc = a_ctx.shape[1]
    width = heads * LANES
    kb = ATT_HEADS // heads
    vb = 2 * ATT_HEADS // heads
    kern = functools.partial(_attn_kernel, lam_init=lam_init, tq=tq, heads=heads)
    nk = n + nc
    return pl.pallas_call(
        kern,
        grid=(b, ATT_HEADS // heads),
        in_specs=[
            pl.BlockSpec((None, n, width), lambda bi, h: (bi, 0, h)),
            pl.BlockSpec((None, n, width), lambda bi, h: (bi, 0, kb + h)),
            pl.BlockSpec((None, n, width), lambda bi, h: (bi, 0, vb + h)),
            pl.BlockSpec((None, nc, width), lambda bi, h: (bi, 0, kb + h)),
            pl.BlockSpec((None, nc, width), lambda bi, h: (bi, 0, vb + h)),
            pl.BlockSpec(lam_vecs.shape, lambda bi, h: (0, 0)),
            pl.BlockSpec(norm_g.shape, lambda bi, h: (0, 0)),
        ],
        out_specs=pl.BlockSpec((None, n, width), lambda bi, h: (bi, 0, h)),
        out_shape=jax.ShapeDtypeStruct((b, n, ATT_HEADS * ATT_V), BF16),
        scratch_shapes=[
            pltpu.VMEM((heads, nk, LANES), BF16),
            pltpu.VMEM((heads, nk, LANES), BF16),
            pltpu.VMEM((2, tq, nk), F32),
            pltpu.VMEM((2, tq, nk), F32),
            pltpu.VMEM((tq, nk), BF16),
            pltpu.VMEM((tq, nk), BF16),
        ],
        compiler_params=_params(("parallel", "parallel"), 56),
        name="diff_attention",
    )(a_lat, a_lat, a_lat, a_ctx, a_ctx, lam_vecs, norm_g)


def _hgrn_kernel(q_ref, zf_ref, zb_ref, i_ref, g_ref, czf_ref, czb_ref, ci_ref, lb_ref, gn_ref, o_ref,
                 o_scr, qs_scr, u_scr, dl_scr, *, n_lat, n_ctx, slot, rb):
    C = REC_CHUNK
    nct = n_ctx // C
    ncl = n_lat // C
    cpb = rb // C

    def lower_bound(d):
        a = lb_ref[d]
        ex = jnp.exp(a - jnp.max(a, axis=0, keepdims=True))
        return jnp.sum(ex[:slot + 1], axis=0, keepdims=True) / jnp.sum(ex, axis=0, keepdims=True)

    lb = (lower_bound(0), lower_bound(1))
    r_i = lax.broadcasted_iota(jnp.int32, (rb, rb), 0)
    c_i = lax.broadcasted_iota(jnp.int32, (rb, rb), 1)
    same = (r_i // C) == (c_i // C)
    masks = (same & (c_i <= r_i), same & (c_i >= r_i))

    tri = masks[0].astype(BF16)

    def chunk_rows(a, offset):
        return jnp.concatenate(
            [jnp.broadcast_to(a[c * C + offset:c * C + offset + 1, :], (C, LANES)) for c in range(cpb)], axis=0)

    def gate(z, lower):
        f = lower + (1.0 - lower) * jax.nn.sigmoid(z)
        return 1.0 - f, jnp.log(f)

    def prep(zf, zb, v, q, base, blk, rows):
        kk_f, lg_f = gate(zf, lb[0])
        kk_b, lg_b = gate(zb, lb[1])
        lg = jnp.concatenate([lg_f, lg_b], axis=1)
        hi = lg.astype(BF16)
        lo = (lg - hi.astype(F32)).astype(BF16)
        pre = _dot(tri, hi) + _dot(tri, lo)
        cum_f = pre[:, :LANES]
        last_f = chunk_rows(cum_f, C - 1)
        last_b = chunk_rows(pre[:, LANES:], C - 1)
        cum_b = last_b - pre[:, LANES:] + lg_b
        kd = jnp.concatenate([kk_f * jnp.exp(last_f - cum_f), kk_b * jnp.exp(last_b - cum_b)], axis=1).astype(BF16)
        for c in range(cpb):
            idx = base + blk * cpb + c
            u = _dot_tn(v[c * C:(c + 1) * C], kd[c * C:(c + 1) * C])
            u_scr[0, idx] = u[:, :LANES]
            u_scr[1, idx] = u[:, LANES:]
            dl_scr[0, idx] = jnp.exp(last_f[c * C:c * C + 8, :])
            dl_scr[1, idx] = jnp.exp(last_b[c * C:c * C + 8, :])
        if q is None:
            return None

        def intra(d, cum, kk, mid):
            ref = chunk_rows(cum, mid)
            dlt = cum - ref
            qe = q * jnp.exp(dlt)
            ke = kk * jnp.exp(-dlt)
            qs_scr[d, rows, :] = (qe * jnp.exp(ref)).astype(BF16)
            return jnp.where(masks[d], _dot_nt(qe.astype(BF16), ke.astype(BF16)), 0.0)

        att = intra(0, cum_f, kk_f, C // 2 - 1) + intra(1, cum_b, kk_b, C // 2)
        return _dot(att.astype(BF16), v)

    for blk in range(n_ctx // rb):
        rows = slice(blk * rb, (blk + 1) * rb)
        prep(czf_ref[rows, :], czb_ref[rows, :], ci_ref[rows, :].astype(BF16), None, 0, blk, rows)

    def lat_block(blk, carry):
        rows = pl.ds(pl.multiple_of(blk * rb, rb), rb)
        v = i_ref[rows, :].astype(BF16)
        q = q_ref[rows, :] * (REC_DK ** -0.5)
        o_scr[rows, :] = prep(zf_ref[rows, :], zb_ref[rows, :], v, q, nct, blk, rows)
        return carry

    lax.fori_loop(0, n_lat // rb, lat_block, 0, unroll=8)

    def advance(d, idx, st):
        return st * dl_scr[d, idx, 0:1, :] + u_scr[d, idx]

    st_f = jnp.zeros((REC_DV, REC_DK), F32)
    st_b = jnp.zeros((REC_DV, REC_DK), F32)
    for c in range(nct):
        st_f = advance(0, c, st_f)
        st_b = advance(1, nct - 1 - c, st_b)

    def seq_step(s, carry):
        st_f, st_b = carry
        cf = s
        cb = ncl - 1 - s
        rows_f = pl.ds(pl.multiple_of(cf * C, C), C)
        rows_b = pl.ds(pl.multiple_of(cb * C, C), C)
        o_scr[rows_f, :] += _dot_nt(qs_scr[0, rows_f, :], st_f.astype(BF16))
        o_scr[rows_b, :] += _dot_nt(qs_scr[1, rows_b, :], st_b.astype(BF16))
        return advance(0, nct + cf, st_f), advance(1, nct + cb, st_b)

    lax.fori_loop(0, ncl, seq_step, (st_f, st_b), unroll=16)

    g = g_ref[...]
    o_ref[...] = (_rms(o_scr[...], gn_ref[...]) * (g * jax.nn.sigmoid(g))).astype(o_ref.dtype)


def _hgrn(r_lat, r_ctx, rec_lb, norm_g, slot):
    b, n, _ = r_lat.shape
    nc = r_ctx.shape[1]
    rb = math.gcd(256, math.gcd(n, nc))
    nchunks = (n + nc) // REC_CHUNK
    cols = [k * REC_HEADS for k in range(5)]
    lat_spec = lambda c: pl.BlockSpec((None, n, LANES), lambda bi, h: (bi, 0, c + h))
    ctx_spec = lambda c: pl.BlockSpec((None, nc, LANES), lambda bi, h: (bi, 0, c + h))
    kern = functools.partial(_hgrn_kernel, n_lat=n, n_ctx=nc, slot=slot, rb=rb)
    return pl.pallas_call(
        kern,
        grid=(b, REC_HEADS),
        in_specs=[lat_spec(cols[0]), lat_spec(cols[1]), lat_spec(cols[2]), lat_spec(cols[3]), lat_spec(cols[4]),
                  ctx_spec(cols[1]), ctx_spec(cols[2]), ctx_spec(cols[3]),
                  pl.BlockSpec((2, rec_lb.shape[1], LANES), lambda bi, h: (0, 0, h)),
                  pl.BlockSpec(norm_g.shape, lambda bi, h: (0, 0))],
        out_specs=pl.BlockSpec((None, n, LANES), lambda bi, h: (bi, 0, h)),
        out_shape=jax.ShapeDtypeStruct((b, n, REC_HEADS * REC_DV), BF16),
        scratch_shapes=[
            pltpu.VMEM((n, LANES), F32),
            pltpu.VMEM((2, n, LANES), BF16),
            pltpu.VMEM((2, nchunks, REC_DV, REC_DK), F32),
            pltpu.VMEM((2, nchunks, 8, LANES), F32),
        ],
        compiler_params=_params(("parallel", "parallel"), 48),
        name="hgrn2_bidir",
    )(r_lat, r_lat, r_lat, r_lat, r_lat, r_ctx, r_ctx, r_ctx, rec_lb, norm_g)


def _mix_out_kernel(a_ref, r_ref, w_ref, x_ref, mod_ref, g_ref, o_ref, *, k0, gi):
    ka = a_ref.shape[-1]
    y = _dot(a_ref[...], w_ref[:ka, :]) + _dot(r_ref[...], w_ref[ka:, :])
    o_ref[...] = x_ref[...] + mod_ref[k0 + 2:k0 + 3, :] * _rms(y, g_ref[gi + 1:gi + 2, :])


def _mix_out(oa, orec, w_bf, x, mod3, row_fn, g_l, k0, gi, tm):
    t, d = x.shape
    ka, kr = oa.shape[-1], orec.shape[-1]
    kern = functools.partial(_mix_out_kernel, k0=k0, gi=gi)
    return pl.pallas_call(
        kern,
        grid=(t // tm,),
        in_specs=[
            pl.BlockSpec((tm, ka), lambda i: (i, 0)),
            pl.BlockSpec((tm, kr), lambda i: (i, 0)),
            pl.BlockSpec(w_bf.shape, lambda i: (0, 0)),
            pl.BlockSpec((tm, d), lambda i: (i, 0)),
            pl.BlockSpec((None, N_MOD, d), lambda i: (row_fn(i), 0, 0)),
            pl.BlockSpec(g_l.shape, lambda i: (0, 0)),
        ],
        out_specs=pl.BlockSpec((tm, d), lambda i: (i, 0)),
        out_shape=jax.ShapeDtypeStruct((t, d), F32),
        compiler_params=_params(("parallel",), 48),
        name="mixer_out",
    )(oa, orec, w_bf, x, mod3, g_l)


def _conv_in_kernel(x_ref, mod_ref, g_ref, wb_ref, wc_ref, wv_ref, b_ref, cv_ref, h_ref, *, k0, gi, rc):
    j = pl.program_id(1)

    def weights():
        return wb_ref[...].astype(BF16), wc_ref[...].astype(BF16), wv_ref[...].astype(BF16)

    @pl.when(j == 0)
    def _():
        wb, wc, wv = weights()
        for r in range(0, x_ref.shape[0], rc):
            rows = slice(r, r + rc)
            h = _rms(x_ref[rows, :], g_ref[gi:gi + 1, :]) * (1.0 + mod_ref[k0 + 1:k0 + 2, :]) + mod_ref[k0:k0 + 1, :]
            h = h.astype(BF16)
            h_ref[rows, :] = h
            b_ref[rows, :] = _dot(h, wb)
            cv_ref[rows, :] = _dot(h, wc) * _dot(h, wv)

    @pl.when(j > 0)
    def _():
        wb, wc, wv = weights()
        h = h_ref[...]
        b_ref[...] = _dot(h, wb)
        cv_ref[...] = _dot(h, wc) * _dot(h, wv)


def _conv_in(x, mod3, row_fn, g_l, w, w_idx, k0, gi, tm, tn):
    t, d = x.shape
    width = w.shape[-1] // 3
    nb = width // tn
    kern = functools.partial(_conv_in_kernel, k0=k0, gi=gi, rc=min(tm, 256))
    w_spec = lambda off: pl.BlockSpec((None, d, tn), lambda i, j: (w_idx, 0, off + j))
    return pl.pallas_call(
        kern,
        grid=(t // tm, nb),
        in_specs=[
            pl.BlockSpec((tm, d), lambda i, j: (i, 0), pipeline_mode=pl.Buffered(1 if tm > 1024 else 2)),
            pl.BlockSpec((None, N_MOD, d), lambda i, j: (row_fn(i), 0, 0)),
            pl.BlockSpec(g_l.shape, lambda i, j: (0, 0)),
            w_spec(0), w_spec(nb), w_spec(2 * nb),
        ],
        out_specs=[pl.BlockSpec((tm, tn), lambda i, j: (i, j))] * 2,
        out_shape=[jax.ShapeDtypeStruct((t, width), F32)] * 2,
        scratch_shapes=[pltpu.VMEM((tm, d), BF16)],
        compiler_params=_params(("parallel", "arbitrary"), 60),
        name="conv_in",
    )(x, mod3, g_l, w, w, w)


def _conv_out_kernel(b_ref, cv_ref, prev_ref, next_ref, cw_ref, w_ref, x_ref, mod_ref, g_ref, o_ref,
                     *, k0, gi, seq_tiles):
    i = pl.program_id(0)
    tm = cv_ref.shape[0]
    cv = cv_ref[...]
    row = lax.broadcasted_iota(jnp.int32, (tm, 1), 0)
    prev = jnp.where(i % seq_tiles == 0, 0.0, prev_ref[7:8, :])
    nxt = jnp.where(i % seq_tiles == seq_tiles - 1, 0.0, next_ref[0:1, :])
    up = jnp.where(row == 0, prev, pltpu.roll(cv, 1, axis=0))
    dn = jnp.where(row == tm - 1, nxt, pltpu.roll(cv, tm - 1, axis=0))
    u = cw_ref[0:1, :] * up + cw_ref[1:2, :] * cv + cw_ref[2:3, :] * dn
    y = _dot((b_ref[...] * u).astype(BF16), w_ref[...])
    o_ref[...] = x_ref[...] + mod_ref[k0 + 2:k0 + 3, :] * _rms(y, g_ref[gi + 1:gi + 2, :])


def _conv_out(bg, cv, conv_w, w_bf, x, mod3, row_fn, g_l, k0, gi, tm, seq):
    t, d = x.shape
    width = cv.shape[-1]
    halo = 8
    hb = tm // halo
    last = t // halo - 1
    kern = functools.partial(_conv_out_kernel, k0=k0, gi=gi, seq_tiles=seq // tm)
    return pl.pallas_call(
        kern,
        grid=(t // tm,),
        in_specs=[
            pl.BlockSpec((tm, width), lambda i: (i, 0)),
            pl.BlockSpec((tm, width), lambda i: (i, 0)),
            pl.BlockSpec((halo, width), lambda i: (jnp.maximum(i * hb - 1, 0), 0)),
            pl.BlockSpec((halo, width), lambda i: (jnp.minimum((i + 1) * hb, last), 0)),
            pl.BlockSpec(conv_w.shape, lambda i: (0, 0)),
            pl.BlockSpec(w_bf.shape, lambda i: (0, 0)),
            pl.BlockSpec((tm, d), lambda i: (i, 0)),
            pl.BlockSpec((None, N_MOD, d), lambda i: (row_fn(i), 0, 0)),
            pl.BlockSpec(g_l.shape, lambda i: (0, 0)),
        ],
        out_specs=pl.BlockSpec((tm, d), lambda i: (i, 0)),
        out_shape=jax.ShapeDtypeStruct((t, d), F32),
        compiler_params=_params(("parallel",), 48),
        name="conv_out",
    )(bg, cv, cv, cv, conv_w, w_bf, x, mod3, g_l)


def _tile(n, pref):
    t = pref
    while n % t:
        t //= 2
    return t


def kernel(x, c, ctx, c_ctx, ada_w, ada_b, norm_g, ffn_w_gate, ffn_w_up, ffn_w_down, mix_w_in, mix_w_out,
           diff_lambda, diff_norm_g, rec_norm_g, rec_lb, conv_w_in, conv_w, conv_w_out):
    batch, seq, d = x.shape
    ctx_len = ctx.shape[1]
    depth = ada_w.shape[0]
    ctx_row = batch
    assert batch < MOD_ROWS

    cvec = jnp.concatenate([c, c_ctx[None, :], jnp.zeros((MOD_ROWS - batch - 1, d), F32)], axis=0)
    xl = x.reshape(batch * seq, d)
    xc = ctx.reshape(batch * ctx_len, d)

    tm_l = _tile(seq, 1024)
    tm_c = _tile(batch * ctx_len, 1024)
    lat_row = lambda tm: (lambda i: i // (seq // tm))
    ctx_rowf = lambda i: ctx_row

    for l in range(depth):
        even = l % 2 == 0
        ctx_out = any(j % 2 == 0 for j in range(l + 1, depth))
        ctx_in = even or ctx_out
        g_l = norm_g[l]
        mod3 = _modulation(cvec, ada_w, ada_b, l)
        ffn = lambda xx, row_fn, half, k0, gi, tm: _ffn_half(
            xx, mod3, row_fn, g_l, ffn_w_gate, ffn_w_up, ffn_w_down, l, half, k0, gi, tm)

        xl = ffn(xl, lat_row(tm_l), 0, 0, 0, tm_l)
        if ctx_in:
            xc = ffn(xc, ctx_rowf, 0, 0, 0, tm_c)

        tm_o = _tile(seq, 512)
        tm_p = _tile(seq, 2048)
        if even:
            e = l // 2
            lam_init = LAMBDA_INIT_BASE - LAMBDA_INIT_AMP * math.exp(-LAMBDA_INIT_RATE * l)
            q_cols = k_cols = ATT_HEADS * 2 * ATT_QK
            att_cols = q_cols + k_cols + ATT_HEADS * ATT_V
            tn = 512
            rope = _rope_tables(seq, LANES // ATT_QK, ATT_QK ** -0.5 * LOG2E) + (q_cols, k_cols)
            a_lat, r_lat = _proj(xl, mod3, lat_row(tm_p), g_l, mix_w_in, e, 3, 2, tm_p, tn, att_cols, rope=rope)
            a_ctx, r_ctx = _proj(xc, mod3, ctx_rowf, g_l, mix_w_in, e, 3, 2, tm_c, tn, att_cols)
            lat3 = lambda a: a.reshape(batch, seq, -1)
            ctx3 = lambda a: a.reshape(batch, ctx_len, -1)
            oa = _attention(lat3(a_lat), ctx3(a_ctx), diff_lambda[e], diff_norm_g[e][None, :], lam_init,
                            _tile(seq, 512), 2)
            orec = _hgrn(lat3(r_lat), ctx3(r_ctx), rec_lb, rec_norm_g[e][None, :], e)
            w_bf = mix_w_out[e].astype(BF16)
            xl_new = _mix_out(oa.reshape(batch * seq, -1), orec.reshape(batch * seq, -1), w_bf, xl, mod3,
                              lat_row(tm_o), g_l, 3, 2, tm_o)
            if ctx_out:
                raise NotImplementedError("context mixer output is not needed for this depth")
        else:
            o = l // 2
            bg, cv = _conv_in(xl, mod3, lat_row(tm_p), g_l, conv_w_in, o, 3, 2, tm_p, 256)
            xl_new = _conv_out(bg, cv, conv_w[o], conv_w_out[o].astype(BF16), xl, mod3, lat_row(tm_o), g_l,
                               3, 2, tm_o, seq)
            if ctx_out:
                raise NotImplementedError("context mixer output is not needed for this depth")
        xl = xl_new
        xl = ffn(xl, lat_row(tm_l), 1, 6, 4, tm_l)
    return xl.reshape(batch, seq, d)
```

```python
import functools
import math

import jax
import jax.numpy as jnp
from jax import lax
from jax.experimental import pallas as pl
from jax.experimental.pallas import tpu as pltpu

F32 = jnp.float32
BF16 = jnp.bfloat16

EPS = 1e-6
N_MOD = 9
FFN_RES_WEIGHT = 0.5
GRID_W = 64
ATT_HEADS = 8
ATT_QK = 64
ATT_V = 2 * ATT_QK
ROPE_THETA = 10000.0
ROPE_PAIRS = ATT_QK // 4
LAMBDA_INIT_BASE = 0.8
LAMBDA_INIT_AMP = 0.6
LAMBDA_INIT_RATE = 0.3
REC_HEADS = 8
REC_DK = 128
REC_DV = 128
REC_CHUNK = 64

LANES = 128
MIB = 1024 * 1024
MOD_ROWS = 8
ROW_CHUNK = 256
LOG2E = 1.4426950408889634


def _params(sem, vmem_mib):
    return pltpu.CompilerParams(dimension_semantics=sem, vmem_limit_bytes=vmem_mib * MIB)


def _rms(y, g):
    return y * lax.rsqrt(jnp.mean(y * y, axis=-1, keepdims=True) + EPS) * g


def _dot(a, b):
    return jnp.dot(a, b, preferred_element_type=F32)


def _dot_nt(a, b):
    return lax.dot_general(a, b, (((1,), (1,)), ((), ())), preferred_element_type=F32)


def _dot_tn(a, b):
    return lax.dot_general(a, b, (((0,), (0,)), ((), ())), preferred_element_type=F32)


def _mod_kernel(c_ref, w_ref, b_ref, o_ref):
    cv = c_ref[...]
    s = (cv * jax.nn.sigmoid(cv)).astype(BF16)
    o_ref[...] = _dot(s, w_ref[...].astype(BF16)) + b_ref[...]


def _modulation(cvec, ada_w, ada_b, layer):
    depth, d, n = ada_w.shape
    tn = 1024
    b3 = ada_b.reshape(depth, 1, n)
    out = pl.pallas_call(
        _mod_kernel,
        grid=(n // tn,),
        in_specs=[
            pl.BlockSpec((MOD_ROWS, d), lambda j: (0, 0)),
            pl.BlockSpec((None, d, tn), lambda j: (layer, 0, j)),
            pl.BlockSpec((None, 1, tn), lambda j: (layer, 0, j)),
        ],
        out_specs=pl.BlockSpec((MOD_ROWS, tn), lambda j: (0, j)),
        out_shape=jax.ShapeDtypeStruct((MOD_ROWS, n), F32),
        compiler_params=_params(("arbitrary",), 40),
        name="modulation",
    )(cvec, ada_w, b3)
    return out.reshape(MOD_ROWS, N_MOD, d)


def _ffn_kernel(x_ref, mod_ref, g_ref, wg_ref, wu_ref, wd_ref, o_ref, h_ref, *, k0, gi, nj, tf, tail, rc):
    j = pl.program_id(1)
    row_chunks = [slice(r, r + rc) for r in range(0, x_ref.shape[0], rc)]

    def weights(width):
        w_gu = jnp.concatenate([wg_ref[:, :width].astype(BF16), wu_ref[:, :width].astype(BF16)], axis=1)
        return w_gu, wd_ref[:width, :].astype(BF16)

    def partial_down(h, w):
        gu = _dot(h, w[0])
        width = gu.shape[-1] // 2
        g, u = gu[:, :width], gu[:, width:]
        a = (g * jax.nn.sigmoid(g) * u).astype(BF16)
        return _dot(a, w[1])

    @pl.when(j == 0)
    def _():
        w = weights(tf)
        for rows in row_chunks:
            h = _rms(x_ref[rows, :], g_ref[gi:gi + 1, :]) * (1.0 + mod_ref[k0 + 1:k0 + 2, :]) + mod_ref[k0:k0 + 1, :]
            h = h.astype(BF16)
            h_ref[rows, :] = h
            o_ref[rows, :] = partial_down(h, w)

    @pl.when((j > 0) & (j < nj - 1))
    def _():
        o_ref[...] += partial_down(h_ref[...], weights(tf))

    @pl.when(j == nj - 1)
    def _():
        w = weights(tail)
        gate = FFN_RES_WEIGHT * mod_ref[k0 + 2:k0 + 3, :]
        for rows in row_chunks:
            y = o_ref[rows, :] + partial_down(h_ref[rows, :], w)
            o_ref[rows, :] = x_ref[rows, :] + gate * _rms(y, g_ref[gi + 1:gi + 2, :])


def _ffn_half(x, mod3, row_fn, g_l, wg, wu, wd, layer, half, k0, gi, tm, tf=256):
    t, d = x.shape
    f = wg.shape[-1]
    nj = pl.cdiv(f, tf)
    tail = f - (nj - 1) * tf
    kern = functools.partial(_ffn_kernel, k0=k0, gi=gi, nj=nj, tf=tf, tail=tail, rc=min(tm, ROW_CHUNK))
    return pl.pallas_call(
        kern,
        grid=(t // tm, nj),
        in_specs=[
            pl.BlockSpec((tm, d), lambda i, j: (i, 0)),
            pl.BlockSpec((None, N_MOD, d), lambda i, j: (row_fn(i), 0, 0)),
            pl.BlockSpec(g_l.shape, lambda i, j: (0, 0)),
            pl.BlockSpec((None, None, d, tf), lambda i, j: (layer, half, 0, j)),
            pl.BlockSpec((None, None, d, tf), lambda i, j: (layer, half, 0, j)),
            pl.BlockSpec((None, None, tf, d), lambda i, j: (layer, half, j, 0)),
        ],
        out_specs=pl.BlockSpec((tm, d), lambda i, j: (i, 0)),
        out_shape=jax.ShapeDtypeStruct((t, d), F32),
        scratch_shapes=[pltpu.VMEM((tm, d), BF16)],
        compiler_params=_params(("parallel", "arbitrary"), 56),
        name="ffn_half",
    )(x, mod3, g_l, wg, wu, wd)


def _proj_kernel(*refs, k0, gi, n_lo, rope_blocks, rc):
    if rope_blocks:
        x_ref, mod_ref, g_ref, w_ref, cos_ref, sin_ref, lo_ref, hi_ref, h_ref = refs
    else:
        x_ref, mod_ref, g_ref, w_ref, lo_ref, hi_ref, h_ref = refs
    j = pl.program_id(1)
    tn = w_ref.shape[-1]
    assert n_lo >= 1
    row_chunks = [slice(r, r + rc) for r in range(0, x_ref.shape[0], rc)]

    def rotate(p, rows):
        lane = lax.broadcasted_iota(jnp.int32, (1, tn), 1)
        first = (lane % (2 * ROPE_PAIRS)) < ROPE_PAIRS
        partner = jnp.where(first, pltpu.roll(p, tn - ROPE_PAIRS, axis=1), pltpu.roll(p, ROPE_PAIRS, axis=1))
        reps = (1, tn // LANES)
        return p * jnp.tile(cos_ref[rows, :], reps) + partner * jnp.tile(sin_ref[rows, :], reps)

    @pl.when(j == 0)
    def _():
        w = w_ref[...].astype(BF16)
        for rows in row_chunks:
            h = _rms(x_ref[rows, :], g_ref[gi:gi + 1, :]) * (1.0 + mod_ref[k0 + 1:k0 + 2, :]) + mod_ref[k0:k0 + 1, :]
            h = h.astype(BF16)
            h_ref[rows, :] = h
            p = _dot(h, w)
            lo_ref[rows, :] = (rotate(p, rows) if rope_blocks else p).astype(BF16)

    if rope_blocks > 1:
        @pl.when((j > 0) & (j < rope_blocks))
        def _():
            w = w_ref[...].astype(BF16)
            for rows in row_chunks:
                lo_ref[rows, :] = rotate(_dot(h_ref[rows, :], w), rows).astype(BF16)

    @pl.when((j >= max(rope_blocks, 1)) & (j < n_lo))
    def _():
        lo_ref[...] = _dot(h_ref[...], w_ref[...].astype(BF16)).astype(BF16)

    @pl.when(j >= n_lo)
    def _():
        hi_ref[...] = _dot(h_ref[...], w_ref[...].astype(BF16))


def _proj(x, mod3, row_fn, g_l, w, w_idx, k0, gi, tm, tn, lo_cols, rope=None):
    t, d = x.shape
    n = w.shape[-1]
    n_lo = lo_cols // tn
    in_specs = [
        pl.BlockSpec((tm, d), lambda i, j: (i, 0), pipeline_mode=pl.Buffered(1 if tm > 1024 else 2)),
        pl.BlockSpec((None, N_MOD, d), lambda i, j: (row_fn(i), 0, 0)),
        pl.BlockSpec(g_l.shape, lambda i, j: (0, 0)),
        pl.BlockSpec((None, d, tn), lambda i, j: (w_idx, 0, j)),
    ]
    args = [x, mod3, g_l, w]
    rope_blocks = 0
    if rope is not None:
        cos_t, sin_t, q_cols, k_cols = rope
        seq_tiles = cos_t.shape[1] // tm
        q_blocks = q_cols // tn
        rope_blocks = (q_cols + k_cols) // tn
        in_specs += [pl.BlockSpec((None, tm, LANES),
                                  lambda i, j: (jnp.minimum(j // q_blocks, 1), i % seq_tiles, 0))] * 2
        args += [cos_t, sin_t]
    kern = functools.partial(_proj_kernel, k0=k0, gi=gi, n_lo=n_lo, rope_blocks=rope_blocks, rc=min(tm, ROW_CHUNK))
    return pl.pallas_call(
        kern,
        grid=(t // tm, n // tn),
        in_specs=in_specs,
        out_specs=[pl.BlockSpec((tm, tn), lambda i, j: (i, jnp.minimum(j, n_lo - 1))),
                   pl.BlockSpec((tm, tn), lambda i, j: (i, jnp.maximum(j - n_lo, 0)))],
        out_shape=[jax.ShapeDtypeStruct((t, lo_cols), BF16), jax.ShapeDtypeStruct((t, n - lo_cols), F32)],
        scratch_shapes=[pltpu.VMEM((tm, d), BF16)],
        compiler_params=_params(("parallel", "arbitrary"), 60),
        name="norm_proj",
    )(*args)


def _rope_tables(seq, reps, q_scale):
    n = jnp.arange(seq)
    row = (n // GRID_W).astype(F32)
    col = (n % GRID_W).astype(F32)
    freqs = ROPE_THETA ** (-jnp.arange(ROPE_PAIRS, dtype=F32) / ROPE_PAIRS)

    def half(pos):
        ang = pos[:, None] * freqs
        c, s = jnp.cos(ang), jnp.sin(ang)
        return jnp.concatenate([c, c], axis=-1), jnp.concatenate([-s, s], axis=-1)

    cr, sr = half(row)
    cc, sc = half(col)
    cos_t = jnp.tile(jnp.concatenate([cr, cc], axis=-1), (1, reps))
    sin_t = jnp.tile(jnp.concatenate([sr, sc], axis=-1), (1, reps))
    return jnp.stack([cos_t * q_scale, cos_t]), jnp.stack([sin_t * q_scale, sin_t])


def _attn_kernel(q_ref, kl_ref, vl_ref, kc_ref, vc_ref, lv_ref, gn_ref, o_ref,
                 k_scr, v_scr, s0_scr, s1_scr, w0_scr, w1_scr, *, lam_init, tq, heads):
    n = q_ref.shape[0]
    nc = kc_ref.shape[0]
    tiles = [(h, r) for h in range(heads) for r in range(0, n, tq)]
    assert len(tiles) >= 3
    for h in range(heads):
        cols = slice(h * LANES, (h + 1) * LANES)
        k_scr[h, :nc, :] = kc_ref[:, cols]
        k_scr[h, nc:, :] = kl_ref[:, cols]
        v_scr[h, :nc, :] = vc_ref[:, cols]
        v_scr[h, nc:, :] = vl_ref[:, cols]
    lane = lax.broadcasted_iota(jnp.int32, (1, 2 * ATT_QK), 1)
    lv = lv_ref[...]
    lam = (jnp.exp(jnp.sum(lv[0:1] * lv[1:2], axis=-1, keepdims=True))
           - jnp.exp(jnp.sum(lv[2:3] * lv[3:4], axis=-1, keepdims=True)) + lam_init)
    s_scr = (s0_scr, s1_scr)
    w_scr = (w0_scr, w1_scr)

    def scores(t):
        h, r = tiles[t]
        q = q_ref[r:r + tq, h * LANES:(h + 1) * LANES]
        zero = jnp.zeros_like(q)
        k = k_scr[h]
        s_scr[t % 2][0] = _dot_nt(jnp.where(lane < ATT_QK, q, zero), k)
        s_scr[t % 2][1] = _dot_nt(jnp.where(lane >= ATT_QK, q, zero), k)

    def weights(t):
        def numerators(m):
            s = s_scr[t % 2][m]
            e = jnp.exp2(s - jnp.max(s, axis=-1, keepdims=True))
            return e, 1.0 / jnp.sum(e, axis=-1, keepdims=True)

        e1, inv1 = numerators(0)
        e2, inv2 = numerators(1)
        w_scr[t % 2][...] = (e1 * inv1 - e2 * (lam * inv2)).astype(BF16)

    def outputs(t):
        h, r = tiles[t]
        o = _dot(w_scr[t % 2][...], v_scr[h])
        o_ref[r:r + tq, h * LANES:(h + 1) * LANES] = (_rms(o, gn_ref[...]) * (1.0 - lam_init)).astype(o_ref.dtype)

    last = len(tiles) - 1
    scores(0)
    scores(1)
    weights(0)
    for t in range(2, last + 1):
        scores(t)
        weights(t - 1)
        outputs(t - 2)
    weights(last)
    outputs(last - 1)
    outputs(last)


def _attention(a_lat, a_ctx, lam_vecs, norm_g, lam_init, tq, heads):
    b, n, _ = a_lat.shape
    nc = a_ctx.shape[1]
    width = heads * LANES
    kb = ATT_HEADS // heads
    vb = 2 * ATT_HEADS // heads
    kern = functools.partial(_attn_kernel, lam_init=lam_init, tq=tq, heads=heads)
    nk = n + nc
    return pl.pallas_call(
        kern,
        grid=(b, ATT_HEADS // heads),
        in_specs=[
            pl.BlockSpec((None, n, width), lambda bi, h: (bi, 0, h)),
            pl.BlockSpec((None, n, width), lambda bi, h: (bi, 0, kb + h)),
            pl.BlockSpec((None, n, width), lambda bi, h: (bi, 0, vb + h)),
            pl.BlockSpec((None, nc, width), lambda bi, h: (bi, 0, kb + h)),
            pl.BlockSpec((None, nc, width), lambda bi, h: (bi, 0, vb + h)),
            pl.BlockSpec(lam_vecs.shape, lambda bi, h: (0, 0)),
            pl.BlockSpec(norm_g.shape, lambda bi, h: (0, 0)),
        ],
        out_specs=pl.BlockSpec((None, n, width), lambda bi, h: (bi, 0, h)),
        out_shape=jax.ShapeDtypeStruct((b, n, ATT_HEADS * ATT_V), BF16),
        scratch_shapes=[
            pltpu.VMEM((heads, nk, LANES), BF16),
            pltpu.VMEM((heads, nk, LANES), BF16),
            pltpu.VMEM((2, tq, nk), F32),
            pltpu.VMEM((2, tq, nk), F32),
            pltpu.VMEM((tq, nk), BF16),
            pltpu.VMEM((tq, nk), BF16),
        ],
        compiler_params=_params(("parallel", "parallel"), 56),
        name="diff_attention",
    )(a_lat, a_lat, a_lat, a_ctx, a_ctx, lam_vecs, norm_g)


def _hgrn_kernel(q_ref, zf_ref, zb_ref, i_ref, g_ref, czf_ref, czb_ref, ci_ref, lb_ref, gn_ref, o_ref,
                 o_scr, qs_scr, u_scr, dl_scr, *, n_lat, n_ctx, slot, rb):
    C = REC_CHUNK
    nct = n_ctx // C
    ncl = n_lat // C
    cpb = rb // C

    def lower_bound(d):
        a = lb_ref[d]
        ex = jnp.exp(a - jnp.max(a, axis=0, keepdims=True))
        return jnp.sum(ex[:slot + 1], axis=0, keepdims=True) / jnp.sum(ex, axis=0, keepdims=True)

    lb = (lower_bound(0), lower_bound(1))
    r_i = lax.broadcasted_iota(jnp.int32, (rb, rb), 0)
    c_i = lax.broadcasted_iota(jnp.int32, (rb, rb), 1)
    same = (r_i // C) == (c_i // C)
    masks = (same & (c_i <= r_i), same & (c_i >= r_i))

    tri = masks[0].astype(BF16)

    def chunk_rows(a, offset):
        return jnp.concatenate(
            [jnp.broadcast_to(a[c * C + offset:c * C + offset + 1, :], (C, LANES)) for c in range(cpb)], axis=0)

    def gate(z, lower):
        f = lower + (1.0 - lower) * jax.nn.sigmoid(z)
        return 1.0 - f, jnp.log(f)

    def prep(zf, zb, v, q, base, blk, rows):
        kk_f, lg_f = gate(zf, lb[0])
        kk_b, lg_b = gate(zb, lb[1])
        lg = jnp.concatenate([lg_f, lg_b], axis=1)
        hi = lg.astype(BF16)
        lo = (lg - hi.astype(F32)).astype(BF16)
        pre = _dot(tri, hi) + _dot(tri, lo)
        cum_f = pre[:, :LANES]
        last_f = chunk_rows(cum_f, C - 1)
        last_b = chunk_rows(pre[:, LANES:], C - 1)
        cum_b = last_b - pre[:, LANES:] + lg_b
        kd = jnp.concatenate([kk_f * jnp.exp(last_f - cum_f), kk_b * jnp.exp(last_b - cum_b)], axis=1).astype(BF16)
        for c in range(cpb):
            idx = base + blk * cpb + c
            u = _dot_tn(v[c * C:(c + 1) * C], kd[c * C:(c + 1) * C])
            u_scr[0, idx] = u[:, :LANES]
            u_scr[1, idx] = u[:, LANES:]
            dl_scr[0, idx] = jnp.exp(last_f[c * C:c * C + 8, :])
            dl_scr[1, idx] = jnp.exp(last_b[c * C:c * C + 8, :])
        if q is None:
            return None

        def intra(d, cum, kk, mid):
            ref = chunk_rows(cum, mid)
            dlt = cum - ref
            qe = q * jnp.exp(dlt)
            ke = kk * jnp.exp(-dlt)
            qs_scr[d, rows, :] = (qe * jnp.exp(ref)).astype(BF16)
            return jnp.where(masks[d], _dot_nt(qe.astype(BF16), ke.astype(BF16)), 0.0)

        att = intra(0, cum_f, kk_f, C // 2 - 1) + intra(1, cum_b, kk_b, C // 2)
        return _dot(att.astype(BF16), v)

    for blk in range(n_ctx // rb):
        rows = slice(blk * rb, (blk + 1) * rb)
        prep(czf_ref[rows, :], czb_ref[rows, :], ci_ref[rows, :].astype(BF16), None, 0, blk, rows)

    def lat_block(blk, carry):
        rows = pl.ds(pl.multiple_of(blk * rb, rb), rb)
        v = i_ref[rows, :].astype(BF16)
        q = q_ref[rows, :] * (REC_DK ** -0.5)
        o_scr[rows, :] = prep(zf_ref[rows, :], zb_ref[rows, :], v, q, nct, blk, rows)
        return carry

    lax.fori_loop(0, n_lat // rb, lat_block, 0, unroll=8)

    def advance(d, idx, st):
        return st * dl_scr[d, idx, 0:1, :] + u_scr[d, idx]

    st_f = jnp.zeros((REC_DV, REC_DK), F32)
    st_b = jnp.zeros((REC_DV, REC_DK), F32)
    for c in range(nct):
        st_f = advance(0, c, st_f)
        st_b = advance(1, nct - 1 - c, st_b)

    def seq_step(s, carry):
        st_f, st_b = carry
        cf = s
        cb = ncl - 1 - s
        rows_f = pl.ds(pl.multiple_of(cf * C, C), C)
        rows_b = pl.ds(pl.multiple_of(cb * C, C), C)
        o_scr[rows_f, :] += _dot_nt(qs_scr[0, rows_f, :], st_f.astype(BF16))
        o_scr[rows_b, :] += _dot_nt(qs_scr[1, rows_b, :], st_b.astype(BF16))
        return advance(0, nct + cf, st_f), advance(1, nct + cb, st_b)

    lax.fori_loop(0, ncl, seq_step, (st_f, st_b), unroll=32)

    g = g_ref[...]
    o_ref[...] = (_rms(o_scr[...], gn_ref[...]) * (g * jax.nn.sigmoid(g))).astype(o_ref.dtype)


def _hgrn(r_lat, r_ctx, rec_lb, norm_g, slot):
    b, n, _ = r_lat.shape
    nc = r_ctx.shape[1]
    rb = math.gcd(256, math.gcd(n, nc))
    nchunks = (n + nc) // REC_CHUNK
    cols = [k * REC_HEADS for k in range(5)]
    lat_spec = lambda c: pl.BlockSpec((None, n, LANES), lambda bi, h: (bi, 0, c + h))
    ctx_spec = lambda c: pl.BlockSpec((None, nc, LANES), lambda bi, h: (bi, 0, c + h))
    kern = functools.partial(_hgrn_kernel, n_lat=n, n_ctx=nc, slot=slot, rb=rb)
    return pl.pallas_call(
        kern,
        grid=(b, REC_HEADS),
        in_specs=[lat_spec(cols[0]), lat_spec(cols[1]), lat_spec(cols[2]), lat_spec(cols[3]), lat_spec(cols[4]),
                  ctx_spec(cols[1]), ctx_spec(cols[2]), ctx_spec(cols[3]),
                  pl.BlockSpec((2, rec_lb.shape[1], LANES), lambda bi, h: (0, 0, h)),
                  pl.BlockSpec(norm_g.shape, lambda bi, h: (0, 0))],
        out_specs=pl.BlockSpec((None, n, LANES), lambda bi, h: (bi, 0, h)),
        out_shape=jax.ShapeDtypeStruct((b, n, REC_HEADS * REC_DV), BF16),
        scratch_shapes=[
            pltpu.VMEM((n, LANES), F32),
            pltpu.VMEM((2, n, LANES), BF16),
            pltpu.VMEM((2, nchunks, REC_DV, REC_DK), F32),
            pltpu.VMEM((2, nchunks, 8, LANES), F32),
        ],
        compiler_params=_params(("parallel", "parallel"), 48),
        name="hgrn2_bidir",
    )(r_lat, r_lat, r_lat, r_lat, r_lat, r_ctx, r_ctx, r_ctx, rec_lb, norm_g)


def _mix_out_kernel(a_ref, r_ref, w_ref, x_ref, mod_ref, g_ref, o_ref, *, k0, gi):
    ka = a_ref.shape[-1]
    y = _dot(a_ref[...], w_ref[:ka, :]) + _dot(r_ref[...], w_ref[ka:, :])
    o_ref[...] = x_ref[...] + mod_ref[k0 + 2:k0 + 3, :] * _rms(y, g_ref[gi + 1:gi + 2, :])


def _mix_out(oa, orec, w_bf, x, mod3, row_fn, g_l, k0, gi, tm):
    t, d = x.shape
    ka, kr = oa.shape[-1], orec.shape[-1]
    kern = functools.partial(_mix_out_kernel, k0=k0, gi=gi)
    return pl.pallas_call(
        kern,
        grid=(t // tm,),
        in_specs=[
            pl.BlockSpec((tm, ka), lambda i: (i, 0)),
            pl.BlockSpec((tm, kr), lambda i: (i, 0)),
            pl.BlockSpec(w_bf.shape, lambda i: (0, 0)),
            pl.BlockSpec((tm, d), lambda i: (i, 0)),
            pl.BlockSpec((None, N_MOD, d), lambda i: (row_fn(i), 0, 0)),
            pl.BlockSpec(g_l.shape, lambda i: (0, 0)),
        ],
        out_specs=pl.BlockSpec((tm, d), lambda i: (i, 0)),
        out_shape=jax.ShapeDtypeStruct((t, d), F32),
        compiler_params=_params(("parallel",), 48),
        name="mixer_out",
    )(oa, orec, w_bf, x, mod3, g_l)


def _conv_in_kernel(x_ref, mod_ref, g_ref, wb_ref, wc_ref, wv_ref, b_ref, cv_ref, h_ref, *, k0, gi, rc):
    j = pl.program_id(1)

    def weights():
        return wb_ref[...].astype(BF16), wc_ref[...].astype(BF16), wv_ref[...].astype(BF16)

    @pl.when(j == 0)
    def _():
        wb, wc, wv = weights()
        for r in range(0, x_ref.shape[0], rc):
            rows = slice(r, r + rc)
            h = _rms(x_ref[rows, :], g_ref[gi:gi + 1, :]) * (1.0 + mod_ref[k0 + 1:k0 + 2, :]) + mod_ref[k0:k0 + 1, :]
            h = h.astype(BF16)
            h_ref[rows, :] = h
            b_ref[rows, :] = _dot(h, wb)
            cv_ref[rows, :] = _dot(h, wc) * _dot(h, wv)

    @pl.when(j > 0)
    def _():
        wb, wc, wv = weights()
        h = h_ref[...]
        b_ref[...] = _dot(h, wb)
        cv_ref[...] = _dot(h, wc) * _dot(h, wv)


def _conv_in(x, mod3, row_fn, g_l, w, w_idx, k0, gi, tm, tn):
    t, d = x.shape
    width = w.shape[-1] // 3
    nb = width // tn
    kern = functools.partial(_conv_in_kernel, k0=k0, gi=gi, rc=min(tm, ROW_CHUNK))
    w_spec = lambda off: pl.BlockSpec((None, d, tn), lambda i, j: (w_idx, 0, off + j))
    return pl.pallas_call(
        kern,
        grid=(t // tm, nb),
        in_specs=[
            pl.BlockSpec((tm, d), lambda i, j: (i, 0), pipeline_mode=pl.Buffered(1 if tm > 1024 else 2)),
            pl.BlockSpec((None, N_MOD, d), lambda i, j: (row_fn(i), 0, 0)),
            pl.BlockSpec(g_l.shape, lambda i, j: (0, 0)),
            w_spec(0), w_spec(nb), w_spec(2 * nb),
        ],
        out_specs=[pl.BlockSpec((tm, tn), lambda i, j: (i, j))] * 2,
        out_shape=[jax.ShapeDtypeStruct((t, width), F32)] * 2,
        scratch_shapes=[pltpu.VMEM((tm, d), BF16)],
        compiler_params=_params(("parallel", "arbitrary"), 60),
        name="conv_in",
    )(x, mod3, g_l, w, w, w)


def _conv_out_kernel(b_ref, cv_ref, prev_ref, next_ref, cw_ref, w_ref, x_ref, mod_ref, g_ref, o_ref,
                     *, k0, gi, seq_tiles):
    i = pl.program_id(0)
    tm = cv_ref.shape[0]
    cv = cv_ref[...]
    row = lax.broadcasted_iota(jnp.int32, (tm, 1), 0)
    prev = jnp.where(i % seq_tiles == 0, 0.0, prev_ref[7:8, :])
    nxt = jnp.where(i % seq_tiles == seq_tiles - 1, 0.0, next_ref[0:1, :])
    up = jnp.where(row == 0, prev, pltpu.roll(cv, 1, axis=0))
    dn = jnp.where(row == tm - 1, nxt, pltpu.roll(cv, tm - 1, axis=0))
    u = cw_ref[0:1, :] * up + cw_ref[1:2, :] * cv + cw_ref[2:3, :] * dn
    y = _dot((b_ref[...] * u).astype(BF16), w_ref[...])
    o_ref[...] = x_ref[...] + mod_ref[k0 + 2:k0 + 3, :] * _rms(y, g_ref[gi + 1:gi + 2, :])


def _conv_out(bg, cv, conv_w, w_bf, x, mod3, row_fn, g_l, k0, gi, tm, seq):
    t, d = x.shape
    width = cv.shape[-1]
    halo = 8
    hb = tm // halo
    last = t // halo - 1
    kern = functools.partial(_conv_out_kernel, k0=k0, gi=gi, seq_tiles=seq // tm)
    return pl.pallas_call(
        kern,
        grid=(t // tm,),
        in_specs=[
            pl.BlockSpec((tm, width), lambda i: (i, 0)),
            pl.BlockSpec((tm, width), lambda i: (i, 0)),
            pl.BlockSpec((halo, width), lambda i: (jnp.maximum(i * hb - 1, 0), 0)),
            pl.BlockSpec((halo, width), lambda i: (jnp.minimum((i + 1) * hb, last), 0)),
            pl.BlockSpec(conv_w.shape, lambda i: (0, 0)),
            pl.BlockSpec(w_bf.shape, lambda i: (0, 0)),
            pl.BlockSpec((tm, d), lambda i: (i, 0)),
            pl.BlockSpec((None, N_MOD, d), lambda i: (row_fn(i), 0, 0)),
            pl.BlockSpec(g_l.shape, lambda i: (0, 0)),
        ],
        out_specs=pl.BlockSpec((tm, d), lambda i: (i, 0)),
        out_shape=jax.ShapeDtypeStruct((t, d), F32),
        compiler_params=_params(("parallel",), 48),
        name="conv_out",
    )(bg, cv, cv, cv, conv_w, w_bf, x, mod3, g_l)


def _tile(n, pref):
    t = pref
    while n % t:
        t //= 2
    return t


def kernel(x, c, ctx, c_ctx, ada_w, ada_b, norm_g, ffn_w_gate, ffn_w_up, ffn_w_down, mix_w_in, mix_w_out,
           diff_lambda, diff_norm_g, rec_norm_g, rec_lb, conv_w_in, conv_w, conv_w_out):
    batch, seq, d = x.shape
    ctx_len = ctx.shape[1]
    depth = ada_w.shape[0]
    ctx_row = batch
    assert batch < MOD_ROWS

    cvec = jnp.concatenate([c, c_ctx[None, :], jnp.zeros((MOD_ROWS - batch - 1, d), F32)], axis=0)
    xl = x.reshape(batch * seq, d)
    xc = ctx.reshape(batch * ctx_len, d)

    tm_l = _tile(seq, 1024)
    tm_c = _tile(batch * ctx_len, 1024)
    lat_row = lambda tm: (lambda i: i // (seq // tm))
    ctx_rowf = lambda i: ctx_row

    for l in range(depth):
        even = l % 2 == 0
        ctx_out = any(j % 2 == 0 for j in range(l + 1, depth))
        ctx_in = even or ctx_out
        g_l = norm_g[l]
        mod3 = _modulation(cvec, ada_w, ada_b, l)
        ffn = lambda xx, row_fn, half, k0, gi, tm: _ffn_half(
            xx, mod3, row_fn, g_l, ffn_w_gate, ffn_w_up, ffn_w_down, l, half, k0, gi, tm)

        xl = ffn(xl, lat_row(tm_l), 0, 0, 0, tm_l)
        if ctx_in:
            xc = ffn(xc, ctx_rowf, 0, 0, 0, tm_c)

        tm_o = _tile(seq, 512)
        tm_p = _tile(seq, 2048)
        if even:
            e = l // 2
            lam_init = LAMBDA_INIT_BASE - LAMBDA_INIT_AMP * math.exp(-LAMBDA_INIT_RATE * l)
            q_cols = k_cols = ATT_HEADS * 2 * ATT_QK
            att_cols = q_cols + k_cols + ATT_HEADS * ATT_V
            tn = 512
            rope = _rope_tables(seq, LANES // ATT_QK, ATT_QK ** -0.5 * LOG2E) + (q_cols, k_cols)
            a_lat, r_lat = _proj(xl, mod3, lat_row(tm_p), g_l, mix_w_in, e, 3, 2, tm_p, tn, att_cols, rope=rope)
            a_ctx, r_ctx = _proj(xc, mod3, ctx_rowf, g_l, mix_w_in, e, 3, 2, tm_c, tn, att_cols)
            lat3 = lambda a: a.reshape(batch, seq, -1)
            ctx3 = lambda a: a.reshape(batch, ctx_len, -1)
            oa = _attention(lat3(a_lat), ctx3(a_ctx), diff_lambda[e], diff_norm_g[e][None, :], lam_init,
                            _tile(seq, 512), 2)
            orec = _hgrn(lat3(r_lat), ctx3(r_ctx), rec_lb, rec_norm_g[e][None, :], e)
            w_bf = mix_w_out[e].astype(BF16)
            xl_new = _mix_out(oa.reshape(batch * seq, -1), orec.reshape(batch * seq, -1), w_bf, xl, mod3,
                              lat_row(tm_o), g_l, 3, 2, tm_o)
            if ctx_out:
                raise NotImplementedError("context mixer output is not needed for this depth")
        else:
            o = l // 2
            bg, cv = _conv_in(xl, mod3, lat_row(tm_p), g_l, conv_w_in, o, 3, 2, tm_p, 256)
            xl_new = _conv_out(bg, cv, conv_w[o], conv_w_out[o].astype(BF16), xl, mod3, lat_row(tm_o), g_l,
                               3, 2, tm_o, seq)
            if ctx_out:
                raise NotImplementedError("context mixer output is not needed for this depth")
        xl = xl_new
        xl = ffn(xl, lat_row(tm_l), 1, 6, 4, tm_l)
    return xl.reshape(batch, seq, d)
```

```python
import functools
import math

import jax
import jax.numpy as jnp
from jax import lax
from jax.experimental import pallas as pl
from jax.experimental.pallas import tpu as pltpu

F32 = jnp.float32
BF16 = jnp.bfloat16

EPS = 1e-6
N_MOD = 9
FFN_RES_WEIGHT = 0.5
GRID_W = 64
ATT_HEADS = 8
ATT_QK = 64
ATT_V = 2 * ATT_QK
ROPE_THETA = 10000.0
ROPE_PAIRS = ATT_QK // 4
LAMBDA_INIT_BASE = 0.8
LAMBDA_INIT_AMP = 0.6
LAMBDA_INIT_RATE = 0.3
REC_HEADS = 8
REC_DK = 128
REC_DV = 128
REC_CHUNK = 64

LANES = 128
MIB = 1024 * 1024
MOD_ROWS = 8
ROW_CHUNK = 256
LOG2E = 1.4426950408889634


def _params(sem, vmem_mib):
    return pltpu.CompilerParams(dimension_semantics=sem, vmem_limit_bytes=vmem_mib * MIB)


def _rms(y, g):
    return y * lax.rsqrt(jnp.mean(y * y, axis=-1, keepdims=True) + EPS) * g


def _dot(a, b):
    return jnp.dot(a, b, preferred_element_type=F32)


def _dot_nt(a, b):
    return lax.dot_general(a, b, (((1,), (1,)), ((), ())), preferred_element_type=F32)


def _dot_tn(a, b):
    return lax.dot_general(a, b, (((0,), (0,)), ((), ())), preferred_element_type=F32)


def _mod_kernel(c_ref, w_ref, b_ref, o_ref):
    cv = c_ref[...]
    s = (cv * jax.nn.sigmoid(cv)).astype(BF16)
    o_ref[...] = _dot(s, w_ref[...].astype(BF16)) + b_ref[...]


def _modulation(cvec, ada_w, ada_b, layer):
    depth, d, n = ada_w.shape
    tn = 1024
    b3 = ada_b.reshape(depth, 1, n)
    out = pl.pallas_call(
        _mod_kernel,
        grid=(n // tn,),
        in_specs=[
            pl.BlockSpec((MOD_ROWS, d), lambda j: (0, 0)),
            pl.BlockSpec((None, d, tn), lambda j: (layer, 0, j)),
            pl.BlockSpec((None, 1, tn), lambda j: (layer, 0, j)),
        ],
        out_specs=pl.BlockSpec((MOD_ROWS, tn), lambda j: (0, j)),
        out_shape=jax.ShapeDtypeStruct((MOD_ROWS, n), F32),
        compiler_params=_params(("arbitrary",), 40),
        name="modulation",
    )(cvec, ada_w, b3)
    return out.reshape(MOD_ROWS, N_MOD, d)


def _ffn_kernel(x_ref, mod_ref, g_ref, wg_ref, wu_ref, wd_ref, o_ref, h_ref, *, k0, gi, nj, tf, tail, rc):
    j = pl.program_id(1)
    row_chunks = [slice(r, r + rc) for r in range(0, x_ref.shape[0], rc)]

    def weights(width):
        w_gu = jnp.concatenate([wg_ref[:, :width].astype(BF16), wu_ref[:, :width].astype(BF16)], axis=1)
        return w_gu, wd_ref[:width, :].astype(BF16)

    def partial_down(h, w):
        gu = _dot(h, w[0])
        width = gu.shape[-1] // 2
        g, u = gu[:, :width], gu[:, width:]
        a = (g * jax.nn.sigmoid(g) * u).astype(BF16)
        return _dot(a, w[1])

    @pl.when(j == 0)
    def _():
        w = weights(tf)
        for rows in row_chunks:
            h = _rms(x_ref[rows, :], g_ref[gi:gi + 1, :]) * (1.0 + mod_ref[k0 + 1:k0 + 2, :]) + mod_ref[k0:k0 + 1, :]
            h = h.astype(BF16)
            h_ref[rows, :] = h
            o_ref[rows, :] = partial_down(h, w)

    @pl.when((j > 0) & (j < nj - 1))
    def _():
        o_ref[...] += partial_down(h_ref[...], weights(tf))

    @pl.when(j == nj - 1)
    def _():
        w = weights(tail)
        gate = FFN_RES_WEIGHT * mod_ref[k0 + 2:k0 + 3, :]
        for rows in row_chunks:
            y = o_ref[rows, :] + partial_down(h_ref[rows, :], w)
            o_ref[rows, :] = x_ref[rows, :] + gate * _rms(y, g_ref[gi + 1:gi + 2, :])


def _ffn_half(x, mod3, row_fn, g_l, wg, wu, wd, layer, half, k0, gi, tm, tf=256):
    t, d = x.shape
    f = wg.shape[-1]
    nj = pl.cdiv(f, tf)
    tail = f - (nj - 1) * tf
    kern = functools.partial(_ffn_kernel, k0=k0, gi=gi, nj=nj, tf=tf, tail=tail, rc=min(tm, ROW_CHUNK))
    return pl.pallas_call(
        kern,
        grid=(t // tm, nj),
        in_specs=[
            pl.BlockSpec((tm, d), lambda i, j: (i, 0)),
            pl.BlockSpec((None, N_MOD, d), lambda i, j: (row_fn(i), 0, 0)),
            pl.BlockSpec(g_l.shape, lambda i, j: (0, 0)),
            pl.BlockSpec((None, None, d, tf), lambda i, j: (layer, half, 0, j)),
            pl.BlockSpec((None, None, d, tf), lambda i, j: (layer, half, 0, j)),
            pl.BlockSpec((None, None, tf, d), lambda i, j: (layer, half, j, 0)),
        ],
        out_specs=pl.BlockSpec((tm, d), lambda i, j: (i, 0)),
        out_shape=jax.ShapeDtypeStruct((t, d), F32),
        scratch_shapes=[pltpu.VMEM((tm, d), BF16)],
        compiler_params=_params(("parallel", "arbitrary"), 56),
        name="ffn_half",
    )(x, mod3, g_l, wg, wu, wd)


def _proj_kernel(*refs, k0, gi, n_lo, rope_blocks, rc):
    if rope_blocks:
        x_ref, mod_ref, g_ref, w_ref, cos_ref, sin_ref, lo_ref, hi_ref, h_ref = refs
    else:
        x_ref, mod_ref, g_ref, w_ref, lo_ref, hi_ref, h_ref = refs
    j = pl.program_id(1)
    tn = w_ref.shape[-1]
    assert n_lo >= 1
    row_chunks = [slice(r, r + rc) for r in range(0, x_ref.shape[0], rc)]

    def rotate(p, rows):
        lane = lax.broadcasted_iota(jnp.int32, (1, tn), 1)
        first = (lane % (2 * ROPE_PAIRS)) < ROPE_PAIRS
        partner = jnp.where(first, pltpu.roll(p, tn - ROPE_PAIRS, axis=1), pltpu.roll(p, ROPE_PAIRS, axis=1))
        reps = (1, tn // LANES)
        return p * jnp.tile(cos_ref[rows, :], reps) + partner * jnp.tile(sin_ref[rows, :], reps)

    @pl.when(j == 0)
    def _():
        w = w_ref[...].astype(BF16)
        for rows in row_chunks:
            h = _rms(x_ref[rows, :], g_ref[gi:gi + 1, :]) * (1.0 + mod_ref[k0 + 1:k0 + 2, :]) + mod_ref[k0:k0 + 1, :]
            h = h.astype(BF16)
            h_ref[rows, :] = h
            p = _dot(h, w)
            lo_ref[rows, :] = (rotate(p, rows) if rope_blocks else p).astype(BF16)

    if rope_blocks > 1:
        @pl.when((j > 0) & (j < rope_blocks))
        def _():
            w = w_ref[...].astype(BF16)
            for rows in row_chunks:
                lo_ref[rows, :] = rotate(_dot(h_ref[rows, :], w), rows).astype(BF16)

    @pl.when((j >= max(rope_blocks, 1)) & (j < n_lo))
    def _():
        lo_ref[...] = _dot(h_ref[...], w_ref[...].astype(BF16)).astype(BF16)

    @pl.when(j >= n_lo)
    def _():
        hi_ref[...] = _dot(h_ref[...], w_ref[...].astype(BF16))


def _proj(x, mod3, row_fn, g_l, w, w_idx, k0, gi, tm, tn, lo_cols, rope=None):
    t, d = x.shape
    n = w.shape[-1]
    n_lo = lo_cols // tn
    in_specs = [
        pl.BlockSpec((tm, d), lambda i, j: (i, 0), pipeline_mode=pl.Buffered(1 if tm > 1024 else 2)),
        pl.BlockSpec((None, N_MOD, d), lambda i, j: (row_fn(i), 0, 0)),
        pl.BlockSpec(g_l.shape, lambda i, j: (0, 0)),
        pl.BlockSpec((None, d, tn), lambda i, j: (w_idx, 0, j)),
    ]
    args = [x, mod3, g_l, w]
    rope_blocks = 0
    if rope is not None:
        cos_t, sin_t, q_cols, k_cols = rope
        seq_tiles = cos_t.shape[1] // tm
        q_blocks = q_cols // tn
        rope_blocks = (q_cols + k_cols) // tn
        in_specs += [pl.BlockSpec((None, tm, LANES),
                                  lambda i, j: (jnp.minimum(j // q_blocks, 1), i % seq_tiles, 0))] * 2
        args += [cos_t, sin_t]
    kern = functools.partial(_proj_kernel, k0=k0, gi=gi, n_lo=n_lo, rope_blocks=rope_blocks, rc=min(tm, ROW_CHUNK))
    return pl.pallas_call(
        kern,
        grid=(t // tm, n // tn),
        in_specs=in_specs,
        out_specs=[pl.BlockSpec((tm, tn), lambda i, j: (i, jnp.minimum(j, n_lo - 1))),
                   pl.BlockSpec((tm, tn), lambda i, j: (i, jnp.maximum(j - n_lo, 0)))],
        out_shape=[jax.ShapeDtypeStruct((t, lo_cols), BF16), jax.ShapeDtypeStruct((t, n - lo_cols), F32)],
        scratch_shapes=[pltpu.VMEM((tm, d), BF16)],
        compiler_params=_params(("parallel", "arbitrary"), 60),
        name="norm_proj",
    )(*args)


def _rope_tables(seq, reps, q_scale):
    n = jnp.arange(seq)
    row = (n // GRID_W).astype(F32)
    col = (n % GRID_W).astype(F32)
    freqs = ROPE_THETA ** (-jnp.arange(ROPE_PAIRS, dtype=F32) / ROPE_PAIRS)

    def half(pos):
        ang = pos[:, None] * freqs
        c, s = jnp.cos(ang), jnp.sin(ang)
        return jnp.concatenate([c, c], axis=-1), jnp.concatenate([-s, s], axis=-1)

    cr, sr = half(row)
    cc, sc = half(col)
    cos_t = jnp.tile(jnp.concatenate([cr, cc], axis=-1), (1, reps))
    sin_t = jnp.tile(jnp.concatenate([sr, sc], axis=-1), (1, reps))
    return jnp.stack([cos_t * q_scale, cos_t]), jnp.stack([sin_t * q_scale, sin_t])


def _attn_kernel(q_ref, kl_ref, vl_ref, kc_ref, vc_ref, lv_ref, gn_ref, o_ref,
                 k_scr, v_scr, s0_scr, s1_scr, w0_scr, w1_scr, *, lam_init, tq, heads):
    n = q_ref.shape[0]
    nc = kc_ref.shape[0]
    tiles = [(h, r) for h in range(heads) for r in range(0, n, tq)]
    assert len(tiles) >= 3
    for h in range(heads):
        cols = slice(h * LANES, (h + 1) * LANES)
        k_scr[h, :nc, :] = kc_ref[:, cols]
        k_scr[h, nc:, :] = kl_ref[:, cols]
        v_scr[h, :nc, :] = vc_ref[:, cols]
        v_scr[h, nc:, :] = vl_ref[:, cols]
    lane = lax.broadcasted_iota(jnp.int32, (1, 2 * ATT_QK), 1)
    lv = lv_ref[...]
    lam = (jnp.exp(jnp.sum(lv[0:1] * lv[1:2], axis=-1, keepdims=True))
           - jnp.exp(jnp.sum(lv[2:3] * lv[3:4], axis=-1, keepdims=True)) + lam_init)
    s_scr = (s0_scr, s1_scr)
    w_scr = (w0_scr, w1_scr)

    def scores(t):
        h, r = tiles[t]
        q = q_ref[r:r + tq, h * LANES:(h + 1) * LANES]
        zero = jnp.zeros_like(q)
        k = k_scr[h]
        s_scr[t % 2][0] = _dot_nt(jnp.where(lane < ATT_QK, q, zero), k)
        s_scr[t % 2][1] = _dot_nt(jnp.where(lane >= ATT_QK, q, zero), k)

    def weights(t):
        def numerators(m):
            s = s_scr[t % 2][m]
            e = jnp.exp2(s - jnp.max(s, axis=-1, keepdims=True))
            return e, 1.0 / jnp.sum(e, axis=-1, keepdims=True)

        e1, inv1 = numerators(0)
        e2, inv2 = numerators(1)
        w_scr[t % 2][...] = (e1 * inv1 - e2 * (lam * inv2)).astype(BF16)

    def outputs(t):
        h, r = tiles[t]
        o = _dot(w_scr[t % 2][...], v_scr[h])
        o_ref[r:r + tq, h * LANES:(h + 1) * LANES] = (_rms(o, gn_ref[...]) * (1.0 - lam_init)).astype(o_ref.dtype)

    last = len(tiles) - 1
    scores(0)
    scores(1)
    weights(0)
    for t in range(2, last + 1):
        scores(t)
        weights(t - 1)
        outputs(t - 2)
    weights(last)
    outputs(last - 1)
    outputs(last)


def _attention(a_lat, a_ctx, lam_vecs, norm_g, lam_init, tq, heads):
    b, n, _ = a_lat.shape
    nc = a_ctx.shape[1]
    width = heads * LANES
    kb = ATT_HEADS // heads
    vb = 2 * ATT_HEADS // heads
    kern = functools.partial(_attn_kernel, lam_init=lam_init, tq=tq, heads=heads)
    nk = n + nc
    return pl.pallas_call(
        kern,
        grid=(b, ATT_HEADS // heads),
        in_specs=[
            pl.BlockSpec((None, n, width), lambda bi, h: (bi, 0, h)),
            pl.BlockSpec((None, n, width), lambda bi, h: (bi, 0, kb + h)),
            pl.BlockSpec((None, n, width), lambda bi, h: (bi, 0, vb + h)),
            pl.BlockSpec((None, nc, width), lambda bi, h: (bi, 0, kb + h)),
            pl.BlockSpec((None, nc, width), lambda bi, h: (bi, 0, vb + h)),
            pl.BlockSpec(lam_vecs.shape, lambda bi, h: (0, 0)),
            pl.BlockSpec(norm_g.shape, lambda bi, h: (0, 0)),
        ],
        out_specs=pl.BlockSpec((None, n, width), lambda bi, h: (bi, 0, h)),
        out_shape=jax.ShapeDtypeStruct((b, n, ATT_HEADS * ATT_V), BF16),
        scratch_shapes=[
            pltpu.VMEM((heads, nk, LANES), BF16),
            pltpu.VMEM((heads, nk, LANES), BF16),
            pltpu.VMEM((2, tq, nk), F32),
            pltpu.VMEM((2, tq, nk), F32),
            pltpu.VMEM((tq, nk), BF16),
            pltpu.VMEM((tq, nk), BF16),
        ],
        compiler_params=_params(("parallel", "parallel"), 56),
        name="diff_attention",
    )(a_lat, a_lat, a_lat, a_ctx, a_ctx, lam_vecs, norm_g)


def _hgrn_kernel(q_ref, zf_ref, zb_ref, i_ref, g_ref, czf_ref, czb_ref, ci_ref, lb_ref, gn_ref, o_ref,
                 o_scr, qs_scr, u_scr, dl_scr, *, heads, **static):
    for h in range(heads):
        cols = slice(h * LANES, (h + 1) * LANES)
        lat = [r.at[:, cols] for r in (q_ref, zf_ref, zb_ref, i_ref, g_ref)]
        ctx = [r.at[:, cols] for r in (czf_ref, czb_ref, ci_ref)]
        _hgrn_head(*lat, *ctx, lb_ref.at[:, :, cols], gn_ref, o_ref.at[:, cols],
                   o_scr.at[h], qs_scr.at[h], u_scr.at[h], dl_scr.at[h], **static)


def _hgrn_head(q_ref, zf_ref, zb_ref, i_ref, g_ref, czf_ref, czb_ref, ci_ref, lb_ref, gn_ref, o_ref,
               o_scr, qs_scr, u_scr, dl_scr, *, n_lat, n_ctx, slot, rb):
    C = REC_CHUNK
    nct = n_ctx // C
    ncl = n_lat // C
    cpb = rb // C

    def lower_bound(d):
        a = lb_ref[d]
        ex = jnp.exp(a - jnp.max(a, axis=0, keepdims=True))
        return jnp.sum(ex[:slot + 1], axis=0, keepdims=True) / jnp.sum(ex, axis=0, keepdims=True)

    lb = (lower_bound(0), lower_bound(1))
    r_i = lax.broadcasted_iota(jnp.int32, (rb, rb), 0)
    c_i = lax.broadcasted_iota(jnp.int32, (rb, rb), 1)
    same = (r_i // C) == (c_i // C)
    masks = (same & (c_i <= r_i), same & (c_i >= r_i))

    tri = masks[0].astype(BF16)

    def chunk_rows(a, offset):
        return jnp.concatenate(
            [jnp.broadcast_to(a[c * C + offset:c * C + offset + 1, :], (C, LANES)) for c in range(cpb)], axis=0)

    def gate(z, lower):
        f = lower + (1.0 - lower) * jax.nn.sigmoid(z)
        return 1.0 - f, jnp.log(f)

    def prep(zf, zb, v, q, base, blk, rows):
        kk_f, lg_f = gate(zf, lb[0])
        kk_b, lg_b = gate(zb, lb[1])
        lg = jnp.concatenate([lg_f, lg_b], axis=1)
        hi = lg.astype(BF16)
        lo = (lg - hi.astype(F32)).astype(BF16)
        pre = _dot(tri, hi) + _dot(tri, lo)
        cum_f = pre[:, :LANES]
        last_f = chunk_rows(cum_f, C - 1)
        last_b = chunk_rows(pre[:, LANES:], C - 1)
        cum_b = last_b - pre[:, LANES:] + lg_b
        kd = jnp.concatenate([kk_f * jnp.exp(last_f - cum_f), kk_b * jnp.exp(last_b - cum_b)], axis=1).astype(BF16)
        for c in range(cpb):
            idx = base + blk * cpb + c
            u = _dot_tn(v[c * C:(c + 1) * C], kd[c * C:(c + 1) * C])
            u_scr[0, idx] = u[:, :LANES]
            u_scr[1, idx] = u[:, LANES:]
            dl_scr[0, idx] = jnp.exp(last_f[c * C:c * C + 8, :])
            dl_scr[1, idx] = jnp.exp(last_b[c * C:c * C + 8, :])
        if q is None:
            return None

        def intra(d, cum, kk, mid):
            ref = chunk_rows(cum, mid)
            dlt = cum - ref
            qe = q * jnp.exp(dlt)
            ke = kk * jnp.exp(-dlt)
            qs_scr[d, rows, :] = (qe * jnp.exp(ref)).astype(BF16)
            return jnp.where(masks[d], _dot_nt(qe.astype(BF16), ke.astype(BF16)), 0.0)

        att = intra(0, cum_f, kk_f, C // 2 - 1) + intra(1, cum_b, kk_b, C // 2)
        return _dot(att.astype(BF16), v)

    for blk in range(n_ctx // rb):
        rows = slice(blk * rb, (blk + 1) * rb)
        prep(czf_ref[rows, :], czb_ref[rows, :], ci_ref[rows, :].astype(BF16), None, 0, blk, rows)

    def lat_block(blk, carry):
        rows = pl.ds(pl.multiple_of(blk * rb, rb), rb)
        v = i_ref[rows, :].astype(BF16)
        q = q_ref[rows, :] * (REC_DK ** -0.5)
        o_scr[rows, :] = prep(zf_ref[rows, :], zb_ref[rows, :], v, q, nct, blk, rows)
        return carry

    lax.fori_loop(0, n_lat // rb, lat_block, 0, unroll=8)

    def advance(d, idx, st):
        return st * dl_scr[d, idx, 0:1, :] + u_scr[d, idx]

    st_f = jnp.zeros((REC_DV, REC_DK), F32)
    st_b = jnp.zeros((REC_DV, REC_DK), F32)
    for c in range(nct):
        st_f = advance(0, c, st_f)
        st_b = advance(1, nct - 1 - c, st_b)

    def seq_step(s, carry):
        st_f, st_b = carry
        cf = s
        cb = ncl - 1 - s
        rows_f = pl.ds(pl.multiple_of(cf * C, C), C)
        rows_b = pl.ds(pl.multiple_of(cb * C, C), C)
        o_scr[rows_f, :] += _dot_nt(qs_scr[0, rows_f, :], st_f.astype(BF16))
        o_scr[rows_b, :] += _dot_nt(qs_scr[1, rows_b, :], st_b.astype(BF16))
        return advance(0, nct + cf, st_f), advance(1, nct + cb, st_b)

    lax.fori_loop(0, ncl, seq_step, (st_f, st_b), unroll=32)

    g = g_ref[...]
    o_ref[...] = (_rms(o_scr[...], gn_ref[...]) * (g * jax.nn.sigmoid(g))).astype(o_ref.dtype)


def _hgrn(r_lat, r_ctx, rec_lb, norm_g, slot, heads):
    b, n, _ = r_lat.shape
    nc = r_ctx.shape[1]
    rb = math.gcd(256, math.gcd(n, nc))
    nchunks = (n + nc) // REC_CHUNK
    groups = REC_HEADS // heads
    width = heads * LANES
    cols = [k * groups for k in range(5)]
    lat_spec = lambda c: pl.BlockSpec((None, n, width), lambda bi, h: (bi, 0, c + h))
    ctx_spec = lambda c: pl.BlockSpec((None, nc, width), lambda bi, h: (bi, 0, c + h))
    kern = functools.partial(_hgrn_kernel, heads=heads, n_lat=n, n_ctx=nc, slot=slot, rb=rb)
    return pl.pallas_call(
        kern,
        grid=(b, groups),
        in_specs=[lat_spec(cols[0]), lat_spec(cols[1]), lat_spec(cols[2]), lat_spec(cols[3]), lat_spec(cols[4]),
                  ctx_spec(cols[1]), ctx_spec(cols[2]), ctx_spec(cols[3]),
                  pl.BlockSpec((2, rec_lb.shape[1], width), lambda bi, h: (0, 0, h)),
                  pl.BlockSpec(norm_g.shape, lambda bi, h: (0, 0))],
        out_specs=pl.BlockSpec((None, n, width), lambda bi, h: (bi, 0, h)),
        out_shape=jax.ShapeDtypeStruct((b, n, REC_HEADS * REC_DV), BF16),
        scratch_shapes=[
            pltpu.VMEM((heads, n, LANES), F32),
            pltpu.VMEM((heads, 2, n, LANES), BF16),
            pltpu.VMEM((heads, 2, nchunks, REC_DV, REC_DK), F32),
            pltpu.VMEM((heads, 2, nchunks, 8, LANES), F32),
        ],
        compiler_params=_params(("parallel", "parallel"), 56),
        name="hgrn2_bidir",
    )(r_lat, r_lat, r_lat, r_lat, r_lat, r_ctx, r_ctx, r_ctx, rec_lb, norm_g)


def _mix_out_kernel(a_ref, r_ref, w_ref, x_ref, mod_ref, g_ref, o_ref, *, k0, gi):
    ka = a_ref.shape[-1]
    y = _dot(a_ref[...], w_ref[:ka, :]) + _dot(r_ref[...], w_ref[ka:, :])
    o_ref[...] = x_ref[...] + mod_ref[k0 + 2:k0 + 3, :] * _rms(y, g_ref[gi + 1:gi + 2, :])


def _mix_out(oa, orec, w_bf, x, mod3, row_fn, g_l, k0, gi, tm):
    t, d = x.shape
    ka, kr = oa.shape[-1], orec.shape[-1]
    kern = functools.partial(_mix_out_kernel, k0=k0, gi=gi)
    return pl.pallas_call(
        kern,
        grid=(t // tm,),
        in_specs=[
            pl.BlockSpec((tm, ka), lambda i: (i, 0)),
            pl.BlockSpec((tm, kr), lambda i: (i, 0)),
            pl.BlockSpec(w_bf.shape, lambda i: (0, 0)),
            pl.BlockSpec((tm, d), lambda i: (i, 0)),
            pl.BlockSpec((None, N_MOD, d), lambda i: (row_fn(i), 0, 0)),
            pl.BlockSpec(g_l.shape, lambda i: (0, 0)),
        ],
        out_specs=pl.BlockSpec((tm, d), lambda i: (i, 0)),
        out_shape=jax.ShapeDtypeStruct((t, d), F32),
        compiler_params=_params(("parallel",), 48),
        name="mixer_out",
    )(oa, orec, w_bf, x, mod3, g_l)


def _conv_in_kernel(x_ref, mod_ref, g_ref, wb_ref, wc_ref, wv_ref, b_ref, cv_ref, h_ref, *, k0, gi, rc):
    j = pl.program_id(1)

    def weights():
        return wb_ref[...].astype(BF16), wc_ref[...].astype(BF16), wv_ref[...].astype(BF16)

    @pl.when(j == 0)
    def _():
        wb, wc, wv = weights()
        for r in range(0, x_ref.shape[0], rc):
            rows = slice(r, r + rc)
            h = _rms(x_ref[rows, :], g_ref[gi:gi + 1, :]) * (1.0 + mod_ref[k0 + 1:k0 + 2, :]) + mod_ref[k0:k0 + 1, :]
            h = h.astype(BF16)
            h_ref[rows, :] = h
            b_ref[rows, :] = _dot(h, wb)
            cv_ref[rows, :] = _dot(h, wc) * _dot(h, wv)

    @pl.when(j > 0)
    def _():
        wb, wc, wv = weights()
        h = h_ref[...]
        b_ref[...] = _dot(h, wb)
        cv_ref[...] = _dot(h, wc) * _dot(h, wv)


def _conv_in(x, mod3, row_fn, g_l, w, w_idx, k0, gi, tm, tn):
    t, d = x.shape
    width = w.shape[-1] // 3
    nb = width // tn
    kern = functools.partial(_conv_in_kernel, k0=k0, gi=gi, rc=min(tm, ROW_CHUNK))
    w_spec = lambda off: pl.BlockSpec((None, d, tn), lambda i, j: (w_idx, 0, off + j))
    return pl.pallas_call(
        kern,
        grid=(t // tm, nb),
        in_specs=[
            pl.BlockSpec((tm, d), lambda i, j: (i, 0), pipeline_mode=pl.Buffered(1 if tm > 1024 else 2)),
            pl.BlockSpec((None, N_MOD, d), lambda i, j: (row_fn(i), 0, 0)),
            pl.BlockSpec(g_l.shape, lambda i, j: (0, 0)),
            w_spec(0), w_spec(nb), w_spec(2 * nb),
        ],
        out_specs=[pl.BlockSpec((tm, tn), lambda i, j: (i, j))] * 2,
        out_shape=[jax.ShapeDtypeStruct((t, width), F32)] * 2,
        scratch_shapes=[pltpu.VMEM((tm, d), BF16)],
        compiler_params=_params(("parallel", "arbitrary"), 60),
        name="conv_in",
    )(x, mod3, g_l, w, w, w)


def _conv_out_kernel(b_ref, cv_ref, prev_ref, next_ref, cw_ref, w_ref, x_ref, mod_ref, g_ref, o_ref,
                     *, k0, gi, seq_tiles):
    i = pl.program_id(0)
    tm = cv_ref.shape[0]
    cv = cv_ref[...]
    row = lax.broadcasted_iota(jnp.int32, (tm, 1), 0)
    prev = jnp.where(i % seq_tiles == 0, 0.0, prev_ref[7:8, :])
    nxt = jnp.where(i % seq_tiles == seq_tiles - 1, 0.0, next_ref[0:1, :])
    up = jnp.where(row == 0, prev, pltpu.roll(cv, 1, axis=0))
    dn = jnp.where(row == tm - 1, nxt, pltpu.roll(cv, tm - 1, axis=0))
    u = cw_ref[0:1, :] * up + cw_ref[1:2, :] * cv + cw_ref[2:3, :] * dn
    y = _dot((b_ref[...] * u).astype(BF16), w_ref[...])
    o_ref[...] = x_ref[...] + mod_ref[k0 + 2:k0 + 3, :] * _rms(y, g_ref[gi + 1:gi + 2, :])


def _conv_out(bg, cv, conv_w, w_bf, x, mod3, row_fn, g_l, k0, gi, tm, seq):
    t, d = x.shape
    width = cv.shape[-1]
    halo = 8
    hb = tm // halo
    last = t // halo - 1
    kern = functools.partial(_conv_out_kernel, k0=k0, gi=gi, seq_tiles=seq // tm)
    return pl.pallas_call(
        kern,
        grid=(t // tm,),
        in_specs=[
            pl.BlockSpec((tm, width), lambda i: (i, 0)),
            pl.BlockSpec((tm, width), lambda i: (i, 0)),
            pl.BlockSpec((halo, width), lambda i: (jnp.maximum(i * hb - 1, 0), 0)),
            pl.BlockSpec((halo, width), lambda i: (jnp.minimum((i + 1) * hb, last), 0)),
            pl.BlockSpec(conv_w.shape, lambda i: (0, 0)),
            pl.BlockSpec(w_bf.shape, lambda i: (0, 0)),
            pl.BlockSpec((tm, d), lambda i: (i, 0)),
            pl.BlockSpec((None, N_MOD, d), lambda i: (row_fn(i), 0, 0)),
            pl.BlockSpec(g_l.shape, lambda i: (0, 0)),
        ],
        out_specs=pl.BlockSpec((tm, d), lambda i: (i, 0)),
        out_shape=jax.ShapeDtypeStruct((t, d), F32),
        compiler_params=_params(("parallel",), 48),
        name="conv_out",
    )(bg, cv, cv, cv, conv_w, w_bf, x, mod3, g_l)


def _tile(n, pref):
    t = pref
    while n % t:
        t //= 2
    return t


def kernel(x, c, ctx, c_ctx, ada_w, ada_b, norm_g, ffn_w_gate, ffn_w_up, ffn_w_down, mix_w_in, mix_w_out,
           diff_lambda, diff_norm_g, rec_norm_g, rec_lb, conv_w_in, conv_w, conv_w_out):
    batch, seq, d = x.shape
    ctx_len = ctx.shape[1]
    depth = ada_w.shape[0]
    ctx_row = batch
    assert batch < MOD_ROWS

    cvec = jnp.concatenate([c, c_ctx[None, :], jnp.zeros((MOD_ROWS - batch - 1, d), F32)], axis=0)
    xl = x.reshape(batch * seq, d)
    xc = ctx.reshape(batch * ctx_len, d)

    tm_l = _tile(seq, 1024)
    tm_c = _tile(batch * ctx_len, 1024)
    lat_row = lambda tm: (lambda i: i // (seq // tm))
    ctx_rowf = lambda i: ctx_row

    for l in range(depth):
        even = l % 2 == 0
        ctx_out = any(j % 2 == 0 for j in range(l + 1, depth))
        ctx_in = even or ctx_out
        g_l = norm_g[l]
        mod3 = _modulation(cvec, ada_w, ada_b, l)
        ffn = lambda xx, row_fn, half, k0, gi, tm: _ffn_half(
            xx, mod3, row_fn, g_l, ffn_w_gate, ffn_w_up, ffn_w_down, l, half, k0, gi, tm)

        xl = ffn(xl, lat_row(tm_l), 0, 0, 0, tm_l)
        if ctx_in:
            xc = ffn(xc, ctx_rowf, 0, 0, 0, tm_c)

        tm_o = _tile(seq, 512)
        tm_p = _tile(seq, 2048)
        if even:
            e = l // 2
            lam_init = LAMBDA_INIT_BASE - LAMBDA_INIT_AMP * math.exp(-LAMBDA_INIT_RATE * l)
            q_cols = k_cols = ATT_HEADS * 2 * ATT_QK
            att_cols = q_cols + k_cols + ATT_HEADS * ATT_V
            tn = 512
            rope = _rope_tables(seq, LANES // ATT_QK, ATT_QK ** -0.5 * LOG2E) + (q_cols, k_cols)
            a_lat, r_lat = _proj(xl, mod3, lat_row(tm_p), g_l, mix_w_in, e, 3, 2, tm_p, tn, att_cols, rope=rope)
            a_ctx, r_ctx = _proj(xc, mod3, ctx_rowf, g_l, mix_w_in, e, 3, 2, tm_c, tn, att_cols)
            lat3 = lambda a: a.reshape(batch, seq, -1)
            ctx3 = lambda a: a.reshape(batch, ctx_len, -1)
            oa = _attention(lat3(a_lat), ctx3(a_ctx), diff_lambda[e], diff_norm_g[e][None, :], lam_init,
                            _tile(seq, 512), 2)
            orec = _hgrn(lat3(r_lat), ctx3(r_ctx), rec_lb, rec_norm_g[e][None, :], e, 2)
            w_bf = mix_w_out[e].astype(BF16)
            xl_new = _mix_out(oa.reshape(batch * seq, -1), orec.reshape(batch * seq, -1), w_bf, xl, mod3,
                              lat_row(tm_o), g_l, 3, 2, tm_o)
            if ctx_out:
                raise NotImplementedError("context mixer output is not needed for this depth")
        else:
            o = l // 2
            bg, cv = _conv_in(xl, mod3, lat_row(tm_p), g_l, conv_w_in, o, 3, 2, tm_p, 256)
            xl_new = _conv_out(bg, cv, conv_w[o], conv_w_out[o].astype(BF16), xl, mod3, lat_row(tm_o), g_l,
                               3, 2, tm_o, seq)
            if ctx_out:
                raise NotImplementedError("context mixer output is not needed for this depth")
        xl = xl_new
        xl = ffn(xl, lat_row(tm_l), 1, 6, 4, tm_l)
    return xl.reshape(batch, seq, d)
```

```python
import functools
import math

import jax
import jax.numpy as jnp
from jax import lax
from jax.experimental import pallas as pl
from jax.experimental.pallas import tpu as pltpu

F32 = jnp.float32
BF16 = jnp.bfloat16

EPS = 1e-6
N_MOD = 9
FFN_RES_WEIGHT = 0.5
GRID_W = 64
ATT_HEADS = 8
ATT_QK = 64
ATT_V = 2 * ATT_QK
ROPE_THETA = 10000.0
ROPE_PAIRS = ATT_QK // 4
LAMBDA_INIT_BASE = 0.8
LAMBDA_INIT_AMP = 0.6
LAMBDA_INIT_RATE = 0.3
REC_HEADS = 8
REC_DK = 128
REC_DV = 128
REC_CHUNK = 64

LANES = 128
MIB = 1024 * 1024
MOD_ROWS = 8
ROW_CHUNK = 256
LOG2E = 1.4426950408889634


def _params(sem, vmem_mib):
    return pltpu.CompilerParams(dimension_semantics=sem, vmem_limit_bytes=vmem_mib * MIB)


def _rms(y, g):
    return y * lax.rsqrt(jnp.mean(y * y, axis=-1, keepdims=True) + EPS) * g


def _dot(a, b):
    return jnp.dot(a, b, preferred_element_type=F32)


def _dot_nt(a, b):
    return lax.dot_general(a, b, (((1,), (1,)), ((), ())), preferred_element_type=F32)


def _dot_tn(a, b):
    return lax.dot_general(a, b, (((0,), (0,)), ((), ())), preferred_element_type=F32)


def _mod_kernel(c_ref, w_ref, b_ref, o_ref):
    cv = c_ref[...]
    s = (cv * jax.nn.sigmoid(cv)).astype(BF16)
    o_ref[...] = _dot(s, w_ref[...].astype(BF16)) + b_ref[...]


def _modulation(cvec, ada_w, ada_b):
    depth, d, n = ada_w.shape
    tn = 1024
    b3 = ada_b.reshape(depth, 1, n)
    out = pl.pallas_call(
        _mod_kernel,
        grid=(depth, n // tn),
        in_specs=[
            pl.BlockSpec((MOD_ROWS, d), lambda l, j: (0, 0)),
            pl.BlockSpec((None, d, tn), lambda l, j: (l, 0, j)),
            pl.BlockSpec((None, 1, tn), lambda l, j: (l, 0, j)),
        ],
        out_specs=pl.BlockSpec((None, MOD_ROWS, tn), lambda l, j: (l, 0, j)),
        out_shape=jax.ShapeDtypeStruct((depth, MOD_ROWS, n), F32),
        compiler_params=_params(("arbitrary", "arbitrary"), 40),
        name="modulation",
    )(cvec, ada_w, b3)
    return out.reshape(depth, MOD_ROWS, N_MOD, d)


def _ffn_kernel(x_ref, mod_ref, g_ref, wg_ref, wu_ref, wd_ref, o_ref, h_ref, *, k0, gi, nj, tf, tail, rc):
    j = pl.program_id(1)
    row_chunks = [slice(r, r + rc) for r in range(0, x_ref.shape[0], rc)]

    def weights(width):
        w_gu = jnp.concatenate([wg_ref[:, :width].astype(BF16), wu_ref[:, :width].astype(BF16)], axis=1)
        return w_gu, wd_ref[:width, :].astype(BF16)

    def partial_down(h, w):
        gu = _dot(h, w[0])
        width = gu.shape[-1] // 2
        g, u = gu[:, :width], gu[:, width:]
        a = (g * jax.nn.sigmoid(g) * u).astype(BF16)
        return _dot(a, w[1])

    @pl.when(j == 0)
    def _():
        w = weights(tf)
        for rows in row_chunks:
            h = _rms(x_ref[rows, :], g_ref[gi:gi + 1, :]) * (1.0 + mod_ref[k0 + 1:k0 + 2, :]) + mod_ref[k0:k0 + 1, :]
            h = h.astype(BF16)
            h_ref[rows, :] = h
            o_ref[rows, :] = partial_down(h, w)

    @pl.when((j > 0) & (j < nj - 1))
    def _():
        o_ref[...] += partial_down(h_ref[...], weights(tf))

    @pl.when(j == nj - 1)
    def _():
        w = weights(tail)
        gate = FFN_RES_WEIGHT * mod_ref[k0 + 2:k0 + 3, :]
        for rows in row_chunks:
            y = o_ref[rows, :] + partial_down(h_ref[rows, :], w)
            o_ref[rows, :] = x_ref[rows, :] + gate * _rms(y, g_ref[gi + 1:gi + 2, :])


def _ffn_half(x, mod3, row_fn, g_l, wg, wu, wd, layer, half, k0, gi, tm, tf=256):
    t, d = x.shape
    f = wg.shape[-1]
    nj = pl.cdiv(f, tf)
    tail = f - (nj - 1) * tf
    kern = functools.partial(_ffn_kernel, k0=k0, gi=gi, nj=nj, tf=tf, tail=tail, rc=min(tm, ROW_CHUNK))
    return pl.pallas_call(
        kern,
        grid=(t // tm, nj),
        in_specs=[
            pl.BlockSpec((tm, d), lambda i, j: (i, 0)),
            pl.BlockSpec((None, N_MOD, d), lambda i, j: (row_fn(i), 0, 0)),
            pl.BlockSpec(g_l.shape, lambda i, j: (0, 0)),
            pl.BlockSpec((None, None, d, tf), lambda i, j: (layer, half, 0, j)),
            pl.BlockSpec((None, None, d, tf), lambda i, j: (layer, half, 0, j)),
            pl.BlockSpec((None, None, tf, d), lambda i, j: (layer, half, j, 0)),
        ],
        out_specs=pl.BlockSpec((tm, d), lambda i, j: (i, 0)),
        out_shape=jax.ShapeDtypeStruct((t, d), F32),
        scratch_shapes=[pltpu.VMEM((tm, d), BF16)],
        compiler_params=_params(("parallel", "arbitrary"), 56),
        name="ffn_half",
    )(x, mod3, g_l, wg, wu, wd)


def _proj_kernel(*refs, k0, gi, n_lo, rope_blocks, rc):
    if rope_blocks:
        x_ref, mod_ref, g_ref, w_ref, cos_ref, sin_ref, lo_ref, hi_ref, h_ref = refs
    else:
        x_ref, mod_ref, g_ref, w_ref, lo_ref, hi_ref, h_ref = refs
    j = pl.program_id(1)
    tn = w_ref.shape[-1]
    assert n_lo >= 1
    row_chunks = [slice(r, r + rc) for r in range(0, x_ref.shape[0], rc)]

    def rotate(p, rows):
        lane = lax.broadcasted_iota(jnp.int32, (1, tn), 1)
        first = (lane % (2 * ROPE_PAIRS)) < ROPE_PAIRS
        partner = jnp.where(first, pltpu.roll(p, tn - ROPE_PAIRS, axis=1), pltpu.roll(p, ROPE_PAIRS, axis=1))
        reps = (1, tn // LANES)
        return p * jnp.tile(cos_ref[rows, :], reps) + partner * jnp.tile(sin_ref[rows, :], reps)

    @pl.when(j == 0)
    def _():
        w = w_ref[...].astype(BF16)
        for rows in row_chunks:
            h = _rms(x_ref[rows, :], g_ref[gi:gi + 1, :]) * (1.0 + mod_ref[k0 + 1:k0 + 2, :]) + mod_ref[k0:k0 + 1, :]
            h = h.astype(BF16)
            h_ref[rows, :] = h
            p = _dot(h, w)
            lo_ref[rows, :] = (rotate(p, rows) if rope_blocks else p).astype(BF16)

    if rope_blocks > 1:
        @pl.when((j > 0) & (j < rope_blocks))
        def _():
            w = w_ref[...].astype(BF16)
            for rows in row_chunks:
                lo_ref[rows, :] = rotate(_dot(h_ref[rows, :], w), rows).astype(BF16)

    @pl.when((j >= max(rope_blocks, 1)) & (j < n_lo))
    def _():
        lo_ref[...] = _dot(h_ref[...], w_ref[...].astype(BF16)).astype(BF16)

    @pl.when(j >= n_lo)
    def _():
        hi_ref[...] = _dot(h_ref[...], w_ref[...].astype(BF16))


def _proj(x, mod3, row_fn, g_l, w, w_idx, k0, gi, tm, tn, lo_cols, rope=None):
    t, d = x.shape
    n = w.shape[-1]
    n_lo = lo_cols // tn
    in_specs = [
        pl.BlockSpec((tm, d), lambda i, j: (i, 0), pipeline_mode=pl.Buffered(1 if tm > 1024 else 2)),
        pl.BlockSpec((None, N_MOD, d), lambda i, j: (row_fn(i), 0, 0)),
        pl.BlockSpec(g_l.shape, lambda i, j: (0, 0)),
        pl.BlockSpec((None, d, tn), lambda i, j: (w_idx, 0, j)),
    ]
    args = [x, mod3, g_l, w]
    rope_blocks = 0
    if rope is not None:
        cos_t, sin_t, q_cols, k_cols = rope
        seq_tiles = cos_t.shape[1] // tm
        q_blocks = q_cols // tn
        rope_blocks = (q_cols + k_cols) // tn
        in_specs += [pl.BlockSpec((None, tm, LANES),
                                  lambda i, j: (jnp.minimum(j // q_blocks, 1), i % seq_tiles, 0))] * 2
        args += [cos_t, sin_t]
    kern = functools.partial(_proj_kernel, k0=k0, gi=gi, n_lo=n_lo, rope_blocks=rope_blocks, rc=min(tm, ROW_CHUNK))
    return pl.pallas_call(
        kern,
        grid=(t // tm, n // tn),
        in_specs=in_specs,
        out_specs=[pl.BlockSpec((tm, tn), lambda i, j: (i, jnp.minimum(j, n_lo - 1))),
                   pl.BlockSpec((tm, tn), lambda i, j: (i, jnp.maximum(j - n_lo, 0)))],
        out_shape=[jax.ShapeDtypeStruct((t, lo_cols), BF16), jax.ShapeDtypeStruct((t, n - lo_cols), F32)],
        scratch_shapes=[pltpu.VMEM((tm, d), BF16)],
        compiler_params=_params(("parallel", "arbitrary"), 60),
        name="norm_proj",
    )(*args)


def _rope_tables(seq, reps, q_scale):
    n = jnp.arange(seq)
    row = (n // GRID_W).astype(F32)
    col = (n % GRID_W).astype(F32)
    freqs = ROPE_THETA ** (-jnp.arange(ROPE_PAIRS, dtype=F32) / ROPE_PAIRS)

    def half(pos):
        ang = pos[:, None] * freqs
        c, s = jnp.cos(ang), jnp.sin(ang)
        return jnp.concatenate([c, c], axis=-1), jnp.concatenate([-s, s], axis=-1)

    cr, sr = half(row)
    cc, sc = half(col)
    cos_t = jnp.tile(jnp.concatenate([cr, cc], axis=-1), (1, reps))
    sin_t = jnp.tile(jnp.concatenate([sr, sc], axis=-1), (1, reps))
    return jnp.stack([cos_t * q_scale, cos_t]), jnp.stack([sin_t * q_scale, sin_t])


def _attn_kernel(q_ref, kl_ref, vl_ref, kc_ref, vc_ref, lv_ref, gn_ref, o_ref,
                 k_scr, v_scr, s0_scr, s1_scr, w0_scr, w1_scr, *, lam_init, tq, heads):
    n = q_ref.shape[0]
    nc = kc_ref.shape[0]
    tiles = [(h, r) for h in range(heads) for r in range(0, n, tq)]
    assert len(tiles) >= 3
    for h in range(heads):
        cols = slice(h * LANES, (h + 1) * LANES)
        k_scr[h, :nc, :] = kc_ref[:, cols]
        k_scr[h, nc:, :] = kl_ref[:, cols]
        v_scr[h, :nc, :] = vc_ref[:, cols]
        v_scr[h, nc:, :] = vl_ref[:, cols]
    lane = lax.broadcasted_iota(jnp.int32, (1, 2 * ATT_QK), 1)
    lv = lv_ref[...]
    lam = (jnp.exp(jnp.sum(lv[0:1] * lv[1:2], axis=-1, keepdims=True))
           - jnp.exp(jnp.sum(lv[2:3] * lv[3:4], axis=-1, keepdims=True)) + lam_init)
    s_scr = (s0_scr, s1_scr)
    w_scr = (w0_scr, w1_scr)

    def scores(t):
        h, r = tiles[t]
        q = q_ref[r:r + tq, h * LANES:(h + 1) * LANES]
        zero = jnp.zeros_like(q)
        k = k_scr[h]
        s_scr[t % 2][0] = _dot_nt(jnp.where(lane < ATT_QK, q, zero), k)
        s_scr[t % 2][1] = _dot_nt(jnp.where(lane >= ATT_QK, q, zero), k)

    def weights(t):
        def numerators(m):
            s = s_scr[t % 2][m]
            e = jnp.exp2(s - jnp.max(s, axis=-1, keepdims=True))
            return e, 1.0 / jnp.sum(e, axis=-1, keepdims=True)

        e1, inv1 = numerators(0)
        e2, inv2 = numerators(1)
        w_scr[t % 2][...] = (e1 * inv1 - e2 * (lam * inv2)).astype(BF16)

    def outputs(t):
        h, r = tiles[t]
        o = _dot(w_scr[t % 2][...], v_scr[h])
        o_ref[r:r + tq, h * LANES:(h + 1) * LANES] = (_rms(o, gn_ref[...]) * (1.0 - lam_init)).astype(o_ref.dtype)

    last = len(tiles) - 1
    scores(0)
    scores(1)
    weights(0)
    for t in range(2, last + 1):
        scores(t)
        weights(t - 1)
        outputs(t - 2)
    weights(last)
    outputs(last - 1)
    outputs(last)


def _attention(a_lat, a_ctx, lam_vecs, norm_g, lam_init, tq, heads):
    b, n, _ = a_lat.shape
    nc = a_ctx.shape[1]
    width = heads * LANES
    kb = ATT_HEADS // heads
    vb = 2 * ATT_HEADS // heads
    kern = functools.partial(_attn_kernel, lam_init=lam_init, tq=tq, heads=heads)
    nk = n + nc
    return pl.pallas_call(
        kern,
        grid=(b, ATT_HEADS // heads),
        in_specs=[
            pl.BlockSpec((None, n, width), lambda bi, h: (bi, 0, h)),
            pl.BlockSpec((None, n, width), lambda bi, h: (bi, 0, kb + h)),
            pl.BlockSpec((None, n, width), lambda bi, h: (bi, 0, vb + h)),
            pl.BlockSpec((None, nc, width), lambda bi, h: (bi, 0, kb + h)),
            pl.BlockSpec((None, nc, width), lambda bi, h: (bi, 0, vb + h)),
            pl.BlockSpec(lam_vecs.shape, lambda bi, h: (0, 0)),
            pl.BlockSpec(norm_g.shape, lambda bi, h: (0, 0)),
        ],
        out_specs=pl.BlockSpec((None, n, width), lambda bi, h: (bi, 0, h)),
        out_shape=jax.ShapeDtypeStruct((b, n, ATT_HEADS * ATT_V), BF16),
        scratch_shapes=[
            pltpu.VMEM((heads, nk, LANES), BF16),
            pltpu.VMEM((heads, nk, LANES), BF16),
            pltpu.VMEM((2, tq, nk), F32),
            pltpu.VMEM((2, tq, nk), F32),
            pltpu.VMEM((tq, nk), BF16),
            pltpu.VMEM((tq, nk), BF16),
        ],
        compiler_params=_params(("parallel", "parallel"), 56),
        name="diff_attention",
    )(a_lat, a_lat, a_lat, a_ctx, a_ctx, lam_vecs, norm_g)


def _hgrn_kernel(q_ref, zf_ref, zb_ref, i_ref, g_ref, czf_ref, czb_ref, ci_ref, lb_ref, gn_ref, o_ref,
                 o_scr, qs_scr, u_scr, dl_scr, *, n_lat, n_ctx, slot, rb):
    C = REC_CHUNK
    nct = n_ctx // C
    ncl = n_lat // C
    cpb = rb // C

    def lower_bound(d):
        a = lb_ref[d]
        ex = jnp.exp(a - jnp.max(a, axis=0, keepdims=True))
        return jnp.sum(ex[:slot + 1], axis=0, keepdims=True) / jnp.sum(ex, axis=0, keepdims=True)

    lb = (lower_bound(0), lower_bound(1))
    r_i = lax.broadcasted_iota(jnp.int32, (rb, rb), 0)
    c_i = lax.broadcasted_iota(jnp.int32, (rb, rb), 1)
    same = (r_i // C) == (c_i // C)
    masks = (same & (c_i <= r_i), same & (c_i >= r_i))

    tri = masks[0].astype(BF16)

    def chunk_rows(a, offset):
        return jnp.concatenate(
            [jnp.broadcast_to(a[c * C + offset:c * C + offset + 1, :], (C, LANES)) for c in range(cpb)], axis=0)

    def gate(z, lower):
        f = lower + (1.0 - lower) * jax.nn.sigmoid(z)
        return 1.0 - f, jnp.log(f)

    def prep(zf, zb, v, q, base, blk, rows):
        kk_f, lg_f = gate(zf, lb[0])
        kk_b, lg_b = gate(zb, lb[1])
        lg = jnp.concatenate([lg_f, lg_b], axis=1)
        hi = lg.astype(BF16)
        lo = (lg - hi.astype(F32)).astype(BF16)
        pre = _dot(tri, hi) + _dot(tri, lo)
        cum_f = pre[:, :LANES]
        last_f = chunk_rows(cum_f, C - 1)
        last_b = chunk_rows(pre[:, LANES:], C - 1)
        cum_b = last_b - pre[:, LANES:] + lg_b
        kd = jnp.concatenate([kk_f * jnp.exp(last_f - cum_f), kk_b * jnp.exp(last_b - cum_b)], axis=1).astype(BF16)
        for c in range(cpb):
            idx = base + blk * cpb + c
            u = _dot_tn(v[c * C:(c + 1) * C], kd[c * C:(c + 1) * C])
            u_scr[0, idx] = u[:, :LANES]
            u_scr[1, idx] = u[:, LANES:]
            dl_scr[0, idx] = jnp.exp(last_f[c * C:c * C + 8, :])
            dl_scr[1, idx] = jnp.exp(last_b[c * C:c * C + 8, :])
        if q is None:
            return None

        def intra(d, cum, kk, mid):
            ref = chunk_rows(cum, mid)
            dlt = cum - ref
            qe = q * jnp.exp(dlt)
            ke = kk * jnp.exp(-dlt)
            qs_scr[d, rows, :] = (qe * jnp.exp(ref)).astype(BF16)
            return jnp.where(masks[d], _dot_nt(qe.astype(BF16), ke.astype(BF16)), 0.0)

        att = intra(0, cum_f, kk_f, C // 2 - 1) + intra(1, cum_b, kk_b, C // 2)
        return _dot(att.astype(BF16), v)

    for blk in range(n_ctx // rb):
        rows = slice(blk * rb, (blk + 1) * rb)
        prep(czf_ref[rows, :], czb_ref[rows, :], ci_ref[rows, :].astype(BF16), None, 0, blk, rows)

    def lat_block(blk, carry):
        rows = pl.ds(pl.multiple_of(blk * rb, rb), rb)
        v = i_ref[rows, :].astype(BF16)
        q = q_ref[rows, :] * (REC_DK ** -0.5)
        o_scr[rows, :] = prep(zf_ref[rows, :], zb_ref[rows, :], v, q, nct, blk, rows)
        return carry

    lax.fori_loop(0, n_lat // rb, lat_block, 0, unroll=8)

    def advance(d, idx, st):
        return st * dl_scr[d, idx, 0:1, :] + u_scr[d, idx]

    st_f = jnp.zeros((REC_DV, REC_DK), F32)
    st_b = jnp.zeros((REC_DV, REC_DK), F32)
    for c in range(nct):
        st_f = advance(0, c, st_f)
        st_b = advance(1, nct - 1 - c, st_b)

    def seq_step(s, carry):
        st_f, st_b = carry
        cf = s
        cb = ncl - 1 - s
        rows_f = pl.ds(pl.multiple_of(cf * C, C), C)
        rows_b = pl.ds(pl.multiple_of(cb * C, C), C)
        o_scr[rows_f, :] += _dot_nt(qs_scr[0, rows_f, :], st_f.astype(BF16))
        o_scr[rows_b, :] += _dot_nt(qs_scr[1, rows_b, :], st_b.astype(BF16))
        return advance(0, nct + cf, st_f), advance(1, nct + cb, st_b)

    lax.fori_loop(0, ncl, seq_step, (st_f, st_b), unroll=32)

    g = g_ref[...]
    o_ref[...] = (_rms(o_scr[...], gn_ref[...]) * (g * jax.nn.sigmoid(g))).astype(o_ref.dtype)


def _hgrn(r_lat, r_ctx, rec_lb, norm_g, slot):
    b, n, _ = r_lat.shape
    nc = r_ctx.shape[1]
    rb = math.gcd(256, math.gcd(n, nc))
    nchunks = (n + nc) // REC_CHUNK
    cols = [k * REC_HEADS for k in range(5)]
    lat_spec = lambda c: pl.BlockSpec((None, n, LANES), lambda bi, h: (bi, 0, c + h))
    ctx_spec = lambda c: pl.BlockSpec((None, nc, LANES), lambda bi, h: (bi, 0, c + h))
    kern = functools.partial(_hgrn_kernel, n_lat=n, n_ctx=nc, slot=slot, rb=rb)
    return pl.pallas_call(
        kern,
        grid=(b, REC_HEADS),
        in_specs=[lat_spec(cols[0]), lat_spec(cols[1]), lat_spec(cols[2]), lat_spec(cols[3]), lat_spec(cols[4]),
                  ctx_spec(cols[1]), ctx_spec(cols[2]), ctx_spec(cols[3]),
                  pl.BlockSpec((2, rec_lb.shape[1], LANES), lambda bi, h: (0, 0, h)),
                  pl.BlockSpec(norm_g.shape, lambda bi, h: (0, 0))],
        out_specs=pl.BlockSpec((None, n, LANES), lambda bi, h: (bi, 0, h)),
        out_shape=jax.ShapeDtypeStruct((b, n, REC_HEADS * REC_DV), BF16),
        scratch_shapes=[
            pltpu.VMEM((n, LANES), F32),
            pltpu.VMEM((2, n, LANES), BF16),
            pltpu.VMEM((2, nchunks, REC_DV, REC_DK), F32),
            pltpu.VMEM((2, nchunks, 8, LANES), F32),
        ],
        compiler_params=_params(("parallel", "parallel"), 48),
        name="hgrn2_bidir",
    )(r_lat, r_lat, r_lat, r_lat, r_lat, r_ctx, r_ctx, r_ctx, rec_lb, norm_g)


def _mix_out_kernel(a_ref, r_ref, w_ref, x_ref, mod_ref, g_ref, o_ref, *, k0, gi):
    ka = a_ref.shape[-1]
    y = _dot(a_ref[...], w_ref[:ka, :]) + _dot(r_ref[...], w_ref[ka:, :])
    o_ref[...] = x_ref[...] + mod_ref[k0 + 2:k0 + 3, :] * _rms(y, g_ref[gi + 1:gi + 2, :])


def _mix_out(oa, orec, w_bf, x, mod3, row_fn, g_l, k0, gi, tm):
    t, d = x.shape
    ka, kr = oa.shape[-1], orec.shape[-1]
    kern = functools.partial(_mix_out_kernel, k0=k0, gi=gi)
    return pl.pallas_call(
        kern,
        grid=(t // tm,),
        in_specs=[
            pl.BlockSpec((tm, ka), lambda i: (i, 0)),
            pl.BlockSpec((tm, kr), lambda i: (i, 0)),
            pl.BlockSpec(w_bf.shape, lambda i: (0, 0)),
            pl.BlockSpec((tm, d), lambda i: (i, 0)),
            pl.BlockSpec((None, N_MOD, d), lambda i: (row_fn(i), 0, 0)),
            pl.BlockSpec(g_l.shape, lambda i: (0, 0)),
        ],
        out_specs=pl.BlockSpec((tm, d), lambda i: (i, 0)),
        out_shape=jax.ShapeDtypeStruct((t, d), F32),
        compiler_params=_params(("parallel",), 48),
        name="mixer_out",
    )(oa, orec, w_bf, x, mod3, g_l)


def _conv_in_kernel(x_ref, mod_ref, g_ref, wb_ref, wc_ref, wv_ref, b_ref, cv_ref, h_ref, *, k0, gi, rc):
    j = pl.program_id(1)

    def weights():
        return wb_ref[...].astype(BF16), wc_ref[...].astype(BF16), wv_ref[...].astype(BF16)

    @pl.when(j == 0)
    def _():
        wb, wc, wv = weights()
        for r in range(0, x_ref.shape[0], rc):
            rows = slice(r, r + rc)
            h = _rms(x_ref[rows, :], g_ref[gi:gi + 1, :]) * (1.0 + mod_ref[k0 + 1:k0 + 2, :]) + mod_ref[k0:k0 + 1, :]
            h = h.astype(BF16)
            h_ref[rows, :] = h
            b_ref[rows, :] = _dot(h, wb)
            cv_ref[rows, :] = _dot(h, wc) * _dot(h, wv)

    @pl.when(j > 0)
    def _():
        wb, wc, wv = weights()
        h = h_ref[...]
        b_ref[...] = _dot(h, wb)
        cv_ref[...] = _dot(h, wc) * _dot(h, wv)


def _conv_in(x, mod3, row_fn, g_l, w, w_idx, k0, gi, tm, tn):
    t, d = x.shape
    width = w.shape[-1] // 3
    nb = width // tn
    kern = functools.partial(_conv_in_kernel, k0=k0, gi=gi, rc=min(tm, ROW_CHUNK))
    w_spec = lambda off: pl.BlockSpec((None, d, tn), lambda i, j: (w_idx, 0, off + j))
    return pl.pallas_call(
        kern,
        grid=(t // tm, nb),
        in_specs=[
            pl.BlockSpec((tm, d), lambda i, j: (i, 0), pipeline_mode=pl.Buffered(1 if tm > 1024 else 2)),
            pl.BlockSpec((None, N_MOD, d), lambda i, j: (row_fn(i), 0, 0)),
            pl.BlockSpec(g_l.shape, lambda i, j: (0, 0)),
            w_spec(0), w_spec(nb), w_spec(2 * nb),
        ],
        out_specs=[pl.BlockSpec((tm, tn), lambda i, j: (i, j))] * 2,
        out_shape=[jax.ShapeDtypeStruct((t, width), F32)] * 2,
        scratch_shapes=[pltpu.VMEM((tm, d), BF16)],
        compiler_params=_params(("parallel", "arbitrary"), 60),
        name="conv_in",
    )(x, mod3, g_l, w, w, w)


def _conv_out_kernel(b_ref, cv_ref, prev_ref, next_ref, cw_ref, w_ref, x_ref, mod_ref, g_ref, o_ref,
                     *, k0, gi, seq_tiles):
    i = pl.program_id(0)
    tm = cv_ref.shape[0]
    cv = cv_ref[...]
    row = lax.broadcasted_iota(jnp.int32, (tm, 1), 0)
    prev = jnp.where(i % seq_tiles == 0, 0.0, prev_ref[7:8, :])
    nxt = jnp.where(i % seq_tiles == seq_tiles - 1, 0.0, next_ref[0:1, :])
    up = jnp.where(row == 0, prev, pltpu.roll(cv, 1, axis=0))
    dn = jnp.where(row == tm - 1, nxt, pltpu.roll(cv, tm - 1, axis=0))
    u = cw_ref[0:1, :] * up + cw_ref[1:2, :] * cv + cw_ref[2:3, :] * dn
    y = _dot((b_ref[...] * u).astype(BF16), w_ref[...])
    o_ref[...] = x_ref[...] + mod_ref[k0 + 2:k0 + 3, :] * _rms(y, g_ref[gi + 1:gi + 2, :])


def _conv_out(bg, cv, conv_w, w_bf, x, mod3, row_fn, g_l, k0, gi, tm, seq):
    t, d = x.shape
    width = cv.shape[-1]
    halo = 8
    hb = tm // halo
    last = t // halo - 1
    kern = functools.partial(_conv_out_kernel, k0=k0, gi=gi, seq_tiles=seq // tm)
    return pl.pallas_call(
        kern,
        grid=(t // tm,),
        in_specs=[
            pl.BlockSpec((tm, width), lambda i: (i, 0)),
            pl.BlockSpec((tm, width), lambda i: (i, 0)),
            pl.BlockSpec((halo, width), lambda i: (jnp.maximum(i * hb - 1, 0), 0)),
            pl.BlockSpec((halo, width), lambda i: (jnp.minimum((i + 1) * hb, last), 0)),
            pl.BlockSpec(conv_w.shape, lambda i: (0, 0)),
            pl.BlockSpec(w_bf.shape, lambda i: (0, 0)),
            pl.BlockSpec((tm, d), lambda i: (i, 0)),
            pl.BlockSpec((None, N_MOD, d), lambda i: (row_fn(i), 0, 0)),
            pl.BlockSpec(g_l.shape, lambda i: (0, 0)),
        ],
        out_specs=pl.BlockSpec((tm, d), lambda i: (i, 0)),
        out_shape=jax.ShapeDtypeStruct((t, d), F32),
        compiler_params=_params(("parallel",), 48),
        name="conv_out",
    )(bg, cv, cv, cv, conv_w, w_bf, x, mod3, g_l)


def _tile(n, pref):
    t = pref
    while n % t:
        t //= 2
    return t


def kernel(x, c, ctx, c_ctx, ada_w, ada_b, norm_g, ffn_w_gate, ffn_w_up, ffn_w_down, mix_w_in, mix_w_out,
           diff_lambda, diff_norm_g, rec_norm_g, rec_lb, conv_w_in, conv_w, conv_w_out):
    batch, seq, d = x.shape
    ctx_len = ctx.shape[1]
    depth = ada_w.shape[0]
    ctx_row = batch
    assert batch < MOD_ROWS

    cvec = jnp.concatenate([c, c_ctx[None, :], jnp.zeros((MOD_ROWS - batch - 1, d), F32)], axis=0)
    xl = x.reshape(batch * seq, d)
    xc = ctx.reshape(batch * ctx_len, d)

    tm_l = _tile(seq, 1024)
    tm_c = _tile(batch * ctx_len, 1024)
    lat_row = lambda tm: (lambda i: i // (seq // tm))
    ctx_rowf = lambda i: ctx_row

    mods = _modulation(cvec, ada_w, ada_b)
    for l in range(depth):
        even = l % 2 == 0
        ctx_out = any(j % 2 == 0 for j in range(l + 1, depth))
        ctx_in = even or ctx_out
        g_l = norm_g[l]
        mod3 = mods[l]
        ffn = lambda xx, row_fn, half, k0, gi, tm: _ffn_half(
            xx, mod3, row_fn, g_l, ffn_w_gate, ffn_w_up, ffn_w_down, l, half, k0, gi, tm)

        xl = ffn(xl, lat_row(tm_l), 0, 0, 0, tm_l)
        if ctx_in:
            xc = ffn(xc, ctx_rowf, 0, 0, 0, tm_c)

        tm_o = _tile(seq, 512)
        tm_p = _tile(seq, 2048)
        if even:
            e = l // 2
            lam_init = LAMBDA_INIT_BASE - LAMBDA_INIT_AMP * math.exp(-LAMBDA_INIT_RATE * l)
            q_cols = k_cols = ATT_HEADS * 2 * ATT_QK
            att_cols = q_cols + k_cols + ATT_HEADS * ATT_V
            tn = 512
            rope = _rope_tables(seq, LANES // ATT_QK, ATT_QK ** -0.5 * LOG2E) + (q_cols, k_cols)
            a_lat, r_lat = _proj(xl, mod3, lat_row(tm_p), g_l, mix_w_in, e, 3, 2, tm_p, tn, att_cols, rope=rope)
            a_ctx, r_ctx = _proj(xc, mod3, ctx_rowf, g_l, mix_w_in, e, 3, 2, tm_c, tn, att_cols)
            lat3 = lambda a: a.reshape(batch, seq, -1)
            ctx3 = lambda a: a.reshape(batch, ctx_len, -1)
            oa = _attention(lat3(a_lat), ctx3(a_ctx), diff_lambda[e], diff_norm_g[e][None, :], lam_init,
                            _tile(seq, 512), 2)
            orec = _hgrn(lat3(r_lat), ctx3(r_ctx), rec_lb, rec_norm_g[e][None, :], e)
            w_bf = mix_w_out[e].astype(BF16)
            xl_new = _mix_out(oa.reshape(batch * seq, -1), orec.reshape(batch * seq, -1), w_bf, xl, mod3,
                              lat_row(tm_o), g_l, 3, 2, tm_o)
            if ctx_out:
                raise NotImplementedError("context mixer output is not needed for this depth")
        else:
            o = l // 2
            bg, cv = _conv_in(xl, mod3, lat_row(tm_p), g_l, conv_w_in, o, 3, 2, tm_p, 256)
            xl_new = _conv_out(bg, cv, conv_w[o], conv_w_out[o].astype(BF16), xl, mod3, lat_row(tm_o), g_l,
                               3, 2, tm_o, seq)
            if ctx_out:
                raise NotImplementedError("context mixer output is not needed for this depth")
        xl = xl_new
        xl = ffn(xl, lat_row(tm_l), 1, 6, 4, tm_l)
    return xl.reshape(batch, seq, d)
```
